```python
import jax, jax.numpy as jnp
from jax import lax
import numpy as np

D_MODEL = 1024
BATCH = 8
SEQ = 8192
DEPTH = 2

CHUNK = 64
HEAD_DIM = 64
SGU_BLOCK = 128
A_GROUPS = 4
A_WIDTH = A_GROUPS * HEAD_DIM
B_HEADS = 6
B_WIDTH = B_HEADS * HEAD_DIM
DECAY_LORA = 64
AAA_LORA = 64
MV_LORA = 32
C_HEADS = 6
C_WIDTH = C_HEADS * HEAD_DIM
LEFT_CHUNKS = 8
BAND = (LEFT_CHUNKS + 1) * CHUNK
REL_CLIP = 256
MIX_WIDTH = A_WIDTH + B_WIDTH + C_WIDTH
A_COLS = 3 * A_WIDTH
B_SHIFT_COLS = 3 * B_WIDTH + DECAY_LORA + AAA_LORA
B_COLS = B_SHIFT_COLS + B_WIDTH
C_COLS = 4 * C_WIDTH
PROJ_COLS = A_COLS + B_COLS + C_COLS
DEEPNORM_ALPHA = (2 * DEPTH) ** 0.25
DEEPNORM_BETA = (8 * DEPTH) ** -0.25
LN_EPS = 1e-5
GN_EPS = 64e-5

kernel_name = 'hybrid_sgu_rwkv7_chunkattn_deepnorm'


def layer_norm(x, g, b, eps=LN_EPS):
    xf = x.astype(jnp.float32)
    mu = jnp.mean(xf, axis=-1, keepdims=True)
    var = jnp.mean(jnp.square(xf - mu), axis=-1, keepdims=True)
    return ((xf - mu) * lax.rsqrt(var + eps)).astype(x.dtype) * g + b


def token_shift(p):
    return jnp.pad(p, ((0, 0), (1, 0), (0, 0)))[:, :-1]


def spatial_gating(u, v, ln_g, ln_b, w_s, b_s):
    bsz, t, _ = u.shape
    nb = t // SGU_BLOCK
    v = layer_norm(v.reshape(bsz, t, A_GROUPS, HEAD_DIM), ln_g.reshape(A_GROUPS, HEAD_DIM), ln_b.reshape(A_GROUPS, HEAD_DIM))
    v = v.reshape(bsz, nb, SGU_BLOCK, A_GROUPS, HEAD_DIM)
    mask = jnp.tril(jnp.ones((SGU_BLOCK, SGU_BLOCK), dtype=bool))
    w = jnp.where(mask[None], w_s, jnp.zeros_like(w_s))
    s = jnp.einsum('gpq,bnqgc->bnpgc', w, v) + b_s.T[None, None, :, :, None]
    return u * s.reshape(bsz, t, A_WIDTH)


def rwkv7_time_mix(p, mu, w0, w2, a0, a2, k_k, k_a, r_k, lnx_g, lnx_b, v_first, vres):
    bsz, t, _ = p.shape
    xs = p + (token_shift(p) - p) * mu
    r, k, v, w_lo, a_lo = jnp.split(xs, [B_WIDTH, 2 * B_WIDTH, 3 * B_WIDTH, 3 * B_WIDTH + DECAY_LORA], axis=-1)
    v_raw = v
    w = -jax.nn.softplus(-(w0 + jnp.tanh(w_lo) @ w2)) - 0.5
    decay = jnp.exp(-jnp.exp(w.astype(jnp.float32)))
    if vres is not None:
        v0, v1, v2 = vres
        v = v + (v_first - v) * jax.nn.sigmoid(v0 + (v @ v1) @ v2)
    a = jax.nn.sigmoid(a0 + a_lo @ a2)
    heads = lambda z: z.reshape(bsz, t, B_HEADS, HEAD_DIM).astype(jnp.float32)
    kk = heads(k * k_k)
    kk = kk / jnp.maximum(jnp.sqrt(jnp.sum(kk * kk, axis=-1, keepdims=True)), 1e-12)
    k = k * (1 + (a - 1) * k_a)
    rh, kh, vh, ah, wh = heads(r), heads(k), heads(v), heads(a), heads(decay)
    a_vec = -kk
    b_vec = kk * ah

    def step(state, inp):
        r_t, w_t, k_t, v_t, a_t, b_t = inp
        sa = jnp.einsum('bhvk,bhk->bhv', state, a_t)
        state = state * w_t[:, :, None, :] + sa[..., None] * b_t[:, :, None, :] + v_t[..., None] * k_t[:, :, None, :]
        return state, jnp.einsum('bhvk,bhk->bhv', state, r_t)

    seq_in = tuple(jnp.moveaxis(z, 1, 0) for z in (rh, wh, kh, vh, a_vec, b_vec))
    s0 = jnp.zeros((bsz, B_HEADS, HEAD_DIM, HEAD_DIM), jnp.float32)
    _, y = lax.scan(step, s0, seq_in)
    y = jnp.moveaxis(y, 0, 1)
    y = layer_norm(y, lnx_g.reshape(B_HEADS, HEAD_DIM), lnx_b.reshape(B_HEADS, HEAD_DIM), eps=GN_EPS)
    y = y + jnp.sum(rh * kh * r_k, axis=-1, keepdims=True) * vh
    return y.reshape(bsz, t, B_WIDTH).astype(p.dtype), v_raw


def chunk_band_attention(q, k, v, rel_table):
    bsz, t, _ = q.shape
    nc = t // CHUNK
    pad = LEFT_CHUNKS * CHUNK
    q = q.reshape(bsz, t, C_HEADS, HEAD_DIM)
    k_pad = jnp.pad(k.reshape(bsz, t, C_HEADS, HEAD_DIM), ((0, 0), (pad, 0), (0, 0), (0, 0)))
    v_pad = jnp.pad(v.reshape(bsz, t, C_HEADS, HEAD_DIM), ((0, 0), (pad, 0), (0, 0), (0, 0)))
    dist = jnp.arange(CHUNK)[:, None] + pad - jnp.arange(BAND)[None, :]
    idx = jnp.clip(dist, -REL_CLIP, REL_CLIP) + REL_CLIP
    bias = rel_table[:, idx].astype(jnp.float32)
    scale = HEAD_DIM ** -0.5

    def one_chunk(n):
        start = n * CHUNK
        qc = lax.dynamic_slice_in_dim(q, start, CHUNK, axis=1)
        kc = lax.dynamic_slice_in_dim(k_pad, start, BAND, axis=1)
        vc = lax.dynamic_slice_in_dim(v_pad, start, BAND, axis=1)
        s = jnp.einsum('bqhd,bkhd->bhqk', qc, kc).astype(jnp.float32) * scale + bias
        valid = (start - pad + jnp.arange(BAND)) >= 0
        s = jnp.where(valid[None, None, None, :], s, -jnp.inf)
        pr = jax.nn.softmax(s, axis=-1).astype(vc.dtype)
        return jnp.einsum('bhqk,bkhd->bqhd', pr, vc)

    out = lax.map(one_chunk, jnp.arange(nc))
    return jnp.moveaxis(out, 0, 1).reshape(bsz, t, C_WIDTH)


def hybrid_layer(x, c, v_first, w_ada, b_ada, w_in, sgu_ln_g, sgu_ln_b, w_spatial, b_spatial, mu_shift, w_decay0, w_decay2, a0, a2, k_k, k_a, r_k, lnx_g, lnx_b, vres, rel_bias, w_out, ln_g, ln_b):
    mod = jax.nn.silu(c) @ w_ada + b_ada
    shift, scale, gate = jnp.split(mod, 3, axis=-1)
    h = x * (1 + scale[:, None, :]) + shift[:, None, :]
    proj = h @ w_in
    pa, pb, pc = jnp.split(proj, [A_COLS, A_COLS + B_COLS], axis=-1)
    u_a, v_a, g_a = jnp.split(pa, 3, axis=-1)
    y_a = jax.nn.silu(g_a) * spatial_gating(u_a, v_a, sgu_ln_g, sgu_ln_b, w_spatial, b_spatial)
    y_b, v_raw = rwkv7_time_mix(pb[..., :B_SHIFT_COLS], mu_shift, w_decay0, w_decay2, a0, a2, k_k, k_a, r_k, lnx_g, lnx_b, v_first, vres)
    y_b = jax.nn.silu(pb[..., B_SHIFT_COLS:]) * y_b
    q_c, k_c, v_c, g_c = jnp.split(pc, 4, axis=-1)
    y_c = jax.nn.silu(g_c) * chunk_band_attention(q_c, k_c, v_c, rel_bias)
    y = jnp.concatenate([y_a, y_b, y_c], axis=-1) @ w_out
    x = layer_norm(DEEPNORM_ALPHA * x + (1 + gate[:, None, :]) * y, ln_g, ln_b)
    return x, v_raw


def setup_inputs(seed: int = 0) -> dict:
    key = jax.random.key(seed)
    ks = jax.random.split(key, 32)
    n = lambda i, shape: jax.random.normal(ks[i], shape, jnp.float32)
    L = DEPTH
    return {
        'x': n(0, (BATCH, SEQ, D_MODEL)),
        'c': n(1, (BATCH, D_MODEL)),
        'w_ada': n(2, (L, D_MODEL, 3 * D_MODEL)) * (0.1 * D_MODEL ** -0.5),
        'b_ada': n(3, (L, 3 * D_MODEL)) * 0.01,
        'w_in': n(4, (L, D_MODEL, PROJ_COLS)) * D_MODEL ** -0.5,
        'sgu_ln_g': 1.0 + 0.02 * n(5, (L, A_WIDTH)),
        'sgu_ln_b': 0.02 * n(6, (L, A_WIDTH)),
        'w_spatial': n(7, (L, A_GROUPS, SGU_BLOCK, SGU_BLOCK)) * SGU_BLOCK ** -0.5,
        'b_spatial': 1.0 + 0.02 * n(8, (L, A_GROUPS, SGU_BLOCK)),
        'mu_shift': jax.random.uniform(ks[9], (L, B_SHIFT_COLS), jnp.float32),
        'w_decay0': jax.random.uniform(ks[10], (L, B_WIDTH), jnp.float32, -6.0, -1.0),
        'w_decay2': n(11, (L, DECAY_LORA, B_WIDTH)) * 0.1,
        'a0': 0.1 * n(12, (L, B_WIDTH)),
        'a2': n(13, (L, AAA_LORA, B_WIDTH)) * (0.5 * AAA_LORA ** -0.5),
        'k_k': 0.85 + 0.02 * n(14, (L, B_WIDTH)),
        'k_a': 1.0 + 0.02 * n(15, (L, B_WIDTH)),
        'r_k': 0.1 * n(16, (L, B_HEADS, HEAD_DIM)),
        'lnx_g': 1.0 + 0.02 * n(17, (L, B_WIDTH)),
        'lnx_b': 0.02 * n(18, (L, B_WIDTH)),
        'v0': 1.0 + 0.1 * n(19, (L - 1, B_WIDTH)),
        'v1': n(20, (L - 1, B_WIDTH, MV_LORA)) * B_WIDTH ** -0.5,
        'v2': n(21, (L - 1, MV_LORA, B_WIDTH)) * (0.1 * MV_LORA ** -0.5),
        'rel_bias': 0.1 * n(22, (L, C_HEADS, 2 * REL_CLIP + 1)),
        'w_out': n(23, (L, MIX_WIDTH, D_MODEL)) * (DEEPNORM_BETA * MIX_WIDTH ** -0.5),
        'ln_g': 1.0 + 0.02 * n(24, (L, D_MODEL)),
        'ln_b': 0.02 * n(25, (L, D_MODEL)),
    }


def reference(x, c, w_ada, b_ada, w_in, sgu_ln_g, sgu_ln_b, w_spatial, b_spatial, mu_shift, w_decay0, w_decay2, a0, a2, k_k, k_a, r_k, lnx_g, lnx_b, v0, v1, v2, rel_bias, w_out, ln_g, ln_b):
    v_first = None
    for i in range(DEPTH):
        vres = (v0[i - 1], v1[i - 1], v2[i - 1]) if i > 0 else None
        x, v_raw = hybrid_layer(x, c, v_first, w_ada[i], b_ada[i], w_in[i], sgu_ln_g[i], sgu_ln_b[i], w_spatial[i], b_spatial[i], mu_shift[i], w_decay0[i], w_decay2[i], a0[i], a2[i], k_k[i], k_a[i], r_k[i], lnx_g[i], lnx_b[i], vres, rel_bias[i], w_out[i], ln_g[i], ln_b[i])
        if i == 0:
            v_first = v_raw
    return x
```

```python
import functools

import jax
import jax.numpy as jnp
from jax import lax
from jax.experimental import pallas as pl
from jax.experimental.pallas import tpu as pltpu

F32 = jnp.float32
BF16 = jnp.bfloat16

D_MODEL = 1024
HEAD_DIM = 64
CHUNK = 64
SGU_BLOCK = 128
A_WIDTH = 256
B_WIDTH = 384
C_WIDTH = 384
LORA = 64
LEFT_CHUNKS = 8
BAND = (LEFT_CHUNKS + 1) * CHUNK
REL_CLIP = 256
A_COLS = 3 * A_WIDTH
B_SHIFT_COLS = 3 * B_WIDTH + 2 * LORA
B_COLS = B_SHIFT_COLS + B_WIDTH
C_COLS = 4 * C_WIDTH
PROJ_COLS = A_COLS + B_COLS + C_COLS
LN_EPS = 1e-5
GN_EPS = 64e-5

LANES = 128
PAIRS = B_WIDTH // LANES
VMEM_LIMIT = 56 * 1024 * 1024

PROJ_TM = 512
RWKV_CHUNKS = 2
ATTN_TQ = LEFT_CHUNKS * CHUNK
OUT_TM = 512

_NN = (((1,), (0,)), ((), ()))
_NT = (((1,), (1,)), ((), ()))
_TN = (((0,), (0,)), ((), ()))


def _dot(a, b, dims=_NN):
    return lax.dot_general(a, b, dims, preferred_element_type=F32)


def _split2(x):
    hi = x.astype(BF16)
    lo = (x - hi.astype(F32)).astype(BF16)
    return hi, lo


def _split3(x):
    hi = x.astype(BF16)
    rem = x - hi.astype(F32)
    mid = rem.astype(BF16)
    lo = (rem - mid.astype(F32)).astype(BF16)
    return hi, mid, lo


def _mm3(a, b, dims=_NN):
    a_hi, a_lo = _split2(a)
    b_hi, b_lo = _split2(b)
    ca, cb = dims[0][0][0], dims[0][1][0]
    lhs = jnp.concatenate([a_hi, a_lo, a_hi], axis=ca)
    rhs = jnp.concatenate([b_hi, b_hi, b_lo], axis=cb)
    return _dot(lhs, rhs, dims)


def _segsum(x, seg2_ref):
    hi, lo = _split2(x)
    return _dot(jnp.concatenate([hi, lo], axis=1), seg2_ref[...])


def _sigmoid(x):
    return 1.0 / (1.0 + jnp.exp(-x))


def _silu(x):
    return x * _sigmoid(x)


def _stack_heads(z, first_head):
    zero = jnp.zeros_like(z)
    return jnp.concatenate([jnp.where(first_head, z, zero), jnp.where(first_head, zero, z)], axis=0)


def _mod_kernel(c_ref, w_ref, b_ref, o_ref):
    c = c_ref[...]
    o_ref[...] = _dot(_silu(c).astype(BF16), w_ref[...]) + b_ref[...]


def _modulation(c, w_ada, b_ada):
    bsz = c.shape[0]
    return pl.pallas_call(
        _mod_kernel,
        out_shape=jax.ShapeDtypeStruct((bsz, 3 * D_MODEL), F32),
        name="adaln_mod",
        compiler_params=pltpu.CompilerParams(vmem_limit_bytes=VMEM_LIMIT),
    )(c, w_ada.astype(BF16), b_ada.reshape(1, -1))


def _proj_kernel(x_ref, sc_ref, sh_ref, w_ref, pa_ref, pbs_ref, gt_ref, q_ref, kv_ref):
    h = (x_ref[0] * sc_ref[0] + sh_ref[0]).astype(BF16)

    def seg(lo, hi):
        return _dot(h, w_ref[:, lo:hi])

    b0 = A_COLS
    c0 = A_COLS + B_COLS
    pa_ref[0] = seg(0, b0)
    pbs_ref[0] = seg(b0, b0 + B_SHIFT_COLS)
    gt_ref[0, :, :B_WIDTH] = seg(b0 + B_SHIFT_COLS, c0)
    gt_ref[0, :, B_WIDTH:] = seg(c0 + 3 * C_WIDTH, c0 + 4 * C_WIDTH)
    q_ref[0] = seg(c0, c0 + C_WIDTH).astype(BF16)
    kv_ref[0] = seg(c0 + C_WIDTH, c0 + 3 * C_WIDTH).astype(BF16)


def _projection(x, scale1p, shift, w_in):
    bsz, t, _ = x.shape
    tm = PROJ_TM
    row = lambda w: pl.BlockSpec((1, tm, w), lambda b, i: (b, i, 0))
    vec = pl.BlockSpec((1, 1, D_MODEL), lambda b, i: (b, 0, 0))
    return pl.pallas_call(
        _proj_kernel,
        grid=(bsz, t // tm),
        in_specs=[row(D_MODEL), vec, vec, pl.BlockSpec((D_MODEL, PROJ_COLS), lambda b, i: (0, 0))],
        out_specs=[row(A_COLS), row(B_SHIFT_COLS), row(B_WIDTH + C_WIDTH), row(C_WIDTH), row(2 * C_WIDTH)],
        out_shape=[
            jax.ShapeDtypeStruct((bsz, t, A_COLS), F32),
            jax.ShapeDtypeStruct((bsz, t, B_SHIFT_COLS), F32),
            jax.ShapeDtypeStruct((bsz, t, B_WIDTH + C_WIDTH), F32),
            jax.ShapeDtypeStruct((bsz, t, C_WIDTH), BF16),
            jax.ShapeDtypeStruct((bsz, t, 2 * C_WIDTH), BF16),
        ],
        name="in_proj",
        compiler_params=pltpu.CompilerParams(
            dimension_semantics=("arbitrary", "arbitrary"), vmem_limit_bytes=VMEM_LIMIT),
    )(x, scale1p, shift, w_in.astype(BF16))


def _rwkv_kernel(has_vres, nch, *refs):
    if has_vres:
        p_ref, vf_ref, v0_ref, v1_ref, v2_ref = refs[:5]
        refs = refs[5:]
    else:
        p_ref = refs[0]
        refs = refs[1:]
    (mu_ref, w0_ref, w2_ref, a0_ref, a2_ref, kk_ref, ka_ref, rk_ref, lng_ref, lnb_ref,
     seg2_ref, tri3_ref, gmask_ref, eye_ref, y_ref, vraw_ref, prev_ref, st_ref) = refs

    @pl.when(pl.program_id(1) == 0)
    def _():
        prev_ref[...] = jnp.zeros_like(prev_ref)
        st_ref[...] = jnp.zeros_like(st_ref)

    p = p_ref[0]
    tb = p.shape[0]
    row = lax.broadcasted_iota(jnp.int32, p.shape, 0)
    p_prev = jnp.where(row == 0, prev_ref[7:8, :], pltpu.roll(p, 1, 0))
    prev_ref[...] = p[tb - 8:tb, :]

    xs = p + (p_prev - p) * mu_ref[...]
    r = xs[:, 0:B_WIDTH]
    k = xs[:, B_WIDTH:2 * B_WIDTH]
    v = xs[:, 2 * B_WIDTH:3 * B_WIDTH]
    w_lo = xs[:, 3 * B_WIDTH:3 * B_WIDTH + LORA]
    a_lo = xs[:, 3 * B_WIDTH + LORA:]
    vraw_ref[0] = v

    zw = w0_ref[...] + _dot(jnp.tanh(w_lo).astype(BF16), w2_ref[...])
    softplus = jnp.maximum(-zw, 0.0) + jnp.log(1.0 + jnp.exp(-jnp.abs(zw)))
    lw = -jnp.exp(-softplus - 0.5)
    if has_vres:
        lora = _dot(_dot(v.astype(BF16), v1_ref[...]).astype(BF16), v2_ref[...])
        v = v + (vf_ref[0] - v) * _sigmoid(v0_ref[...] + lora)
    a = _sigmoid(a0_ref[...] + _dot(a_lo.astype(BF16), a2_ref[...]))
    kk = k * kk_ref[...]
    kk = kk * lax.rsqrt(jnp.maximum(_segsum(kk * kk, seg2_ref), 1e-24))
    k = k * (1.0 + (a - 1.0) * ka_ref[...])
    av = -kk
    bv = kk * a
    bonus = _segsum(r * k * rk_ref[...], seg2_ref) * v

    first_head = lax.broadcasted_iota(jnp.int32, (CHUNK, LANES), 1) < HEAD_DIM
    eye = eye_ref[...]
    gmask = gmask_ref[...]
    for c in range(nch):
        rs = slice(c * CHUNK, (c + 1) * CHUNK)
        lw_c = lw[rs]
        cs = _dot(tri3_ref[...], jnp.concatenate(_split3(lw_c), axis=0))
        cs_end = cs[CHUNK - 1:CHUNK, :]
        at = av[rs] * jnp.exp(cs - lw_c)
        rt = r[rs] * jnp.exp(cs)
        inv = jnp.exp(-cs)
        bt = bv[rs] * inv
        kt = k[rs] * inv
        rem = jnp.exp(cs_end - cs)
        bp = bv[rs] * rem
        kp = k[rs] * rem
        p_end = jnp.exp(cs_end)
        vc = v[rs]
        for q in range(PAIRS):
            ls = slice(q * LANES, (q + 1) * LANES)
            ats, rts, bts, kts, bps, kps, vs = (
                _stack_heads(z[:, ls], first_head) for z in (at, rt, bt, kt, bp, kp, vc))
            g = _mm3(jnp.concatenate([ats, rts], axis=0), jnp.concatenate([bts, kts], axis=0), _NT) * gmask
            a_ab = g[:LANES, :LANES]
            a_ak = g[:LANES, LANES:]
            a_r = g[LANES:, :]
            tinv = eye + a_ab
            pw = a_ab
            for _ in range(5):
                pw = _mm3(pw, pw)
                tinv = tinv + _mm3(tinv, pw)
            wu = _mm3(tinv, jnp.concatenate([ats, _mm3(a_ak, vs)], axis=1))
            w_s = wu[:, :LANES]
            uv = jnp.concatenate([wu[:, LANES:], vs], axis=0)
            qp_s = rts + _mm3(a_r[:, :LANES], w_s)
            y0_s = _mm3(a_r, uv)
            m = _mm3(bps, w_s, _TN) + eye * p_end[:, ls]
            n = _mm3(jnp.concatenate([bps, kps], axis=0), uv, _TN)
            s = st_ref[q]
            y_ref[0, rs, ls] = _mm3(qp_s[:CHUNK] + qp_s[CHUNK:], s) + y0_s[:CHUNK] + y0_s[CHUNK:]
            st_ref[q] = _mm3(m, s) + n

    y = y_ref[0]
    yc = y - _segsum(y, seg2_ref) * (1.0 / HEAD_DIM)
    var = _segsum(yc * yc, seg2_ref) * (1.0 / HEAD_DIM)
    y_ref[0] = yc * lax.rsqrt(var + GN_EPS) * lng_ref[...] + lnb_ref[...] + bonus


def _rwkv(pbs, v_first, prm, vres):
    bsz, t, _ = pbs.shape
    nch = RWKV_CHUNKS
    tb = nch * CHUNK
    has_vres = vres is not None
    row = lambda w: pl.BlockSpec((1, tb, w), lambda b, i: (b, i, 0))
    full = lambda a: pl.BlockSpec(a.shape, lambda b, i: (0,) * a.ndim)
    vec = lambda a: a.reshape(1, -1)

    head = jnp.arange(B_WIDTH) // HEAD_DIM
    seg = (head[:, None] == head[None, :]).astype(BF16)
    seg2 = jnp.concatenate([seg, seg], axis=0)
    tt = jnp.arange(CHUNK)
    tri = (tt[:, None] >= tt[None, :]).astype(BF16)
    tri3 = jnp.concatenate([tri, tri, tri], axis=1)
    idx = jnp.arange(2 * LANES)
    part, blk, pos = idx // LANES, (idx % LANES) // CHUNK, idx % CHUNK
    allowed = (blk[:, None] == blk[None, :]) & (
        (pos[None, :] < pos[:, None]) | ((part[:, None] == 1) & (pos[None, :] == pos[:, None])))
    gmask = allowed.astype(F32)
    eye = jnp.eye(LANES, dtype=F32)

    consts = [vec(prm["mu_shift"]), vec(prm["w_decay0"]), prm["w_decay2"].astype(BF16), vec(prm["a0"]),
              prm["a2"].astype(BF16), vec(prm["k_k"]), vec(prm["k_a"]), vec(prm["r_k"]), vec(prm["lnx_g"]),
              vec(prm["lnx_b"]), seg2, tri3, gmask, eye]
    args = [pbs]
    specs = [row(B_SHIFT_COLS)]
    if has_vres:
        v0, v1, v2 = vres
        extra = [vec(v0), v1.astype(BF16), v2.astype(BF16)]
        args += [v_first] + extra
        specs += [row(B_WIDTH)] + [full(a) for a in extra]
    args += consts
    specs += [full(a) for a in consts]
    return pl.pallas_call(
        functools.partial(_rwkv_kernel, has_vres, nch),
        grid=(bsz, t // tb),
        in_specs=specs,
        out_specs=[row(B_WIDTH), row(B_WIDTH)],
        out_shape=[jax.ShapeDtypeStruct((bsz, t, B_WIDTH), F32)] * 2,
        scratch_shapes=[pltpu.VMEM((8, B_SHIFT_COLS), F32), pltpu.VMEM((PAIRS, LANES, LANES), F32)],
        name="rwkv7_vres" if has_vres else "rwkv7",
        compiler_params=pltpu.CompilerParams(
            dimension_semantics=("arbitrary", "arbitrary"), vmem_limit_bytes=VMEM_LIMIT),
    )(*args)


def _attn_kernel(q_ref, kvp_ref, kvc_ref, bias_ref, o_ref, kv_ref):
    tq = ATTN_TQ
    kv_ref[0:tq] = kvp_ref[0]
    kv_ref[tq:2 * tq] = kvc_ref[0]
    not_first = pl.program_id(1) > 0
    first_head = lax.broadcasted_iota(jnp.int32, (CHUNK, LANES), 1) < HEAD_DIM
    col = lax.broadcasted_iota(jnp.int32, (2 * CHUNK, BAND), 1)
    scale = HEAD_DIM ** -0.5
    for c in range(tq // CHUNK):
        rs = slice(c * CHUNK, (c + 1) * CHUNK)
        band = slice(c * CHUNK, c * CHUNK + BAND)
        valid = (col >= tq - c * CHUNK) | not_first
        for pr in range(PAIRS):
            ls = slice(pr * LANES, (pr + 1) * LANES)
            qs = _stack_heads(q_ref[0, rs, ls], first_head)
            kb = kv_ref[band, ls]
            vb = kv_ref[band, C_WIDTH + pr * LANES:C_WIDTH + (pr + 1) * LANES]
            s = _dot(qs, kb, _NT) * scale + bias_ref[pr]
            s = jnp.where(valid, s, -jnp.inf)
            e = jnp.exp(s - jnp.max(s, axis=-1, keepdims=True))
            o = _dot(e.astype(BF16), vb) / jnp.sum(e, axis=-1, keepdims=True)
            o_ref[0, rs, ls] = jnp.where(first_head, o[:CHUNK], o[CHUNK:])


def _attention(q, kv, rel_table):
    bsz, t, _ = q.shape
    tq = ATTN_TQ
    pad = LEFT_CHUNKS * CHUNK
    dist = jnp.arange(CHUNK)[:, None] + pad - jnp.arange(BAND)[None, :]
    idx = jnp.clip(dist, -REL_CLIP, REL_CLIP) + REL_CLIP
    bias = rel_table[:, idx].astype(F32).reshape(PAIRS, 2 * CHUNK, BAND)
    return pl.pallas_call(
        _attn_kernel,
        grid=(bsz, t // tq),
        in_specs=[
            pl.BlockSpec((1, tq, C_WIDTH), lambda b, i: (b, i, 0)),
            pl.BlockSpec((1, tq, 2 * C_WIDTH), lambda b, i: (b, jnp.maximum(i - 1, 0), 0)),
            pl.BlockSpec((1, tq, 2 * C_WIDTH), lambda b, i: (b, i, 0)),
            pl.BlockSpec(bias.shape, lambda b, i: (0, 0, 0)),
        ],
        out_specs=pl.BlockSpec((1, tq, C_WIDTH), lambda b, i: (b, i, 0)),
        out_shape=jax.ShapeDtypeStruct((bsz, t, C_WIDTH), F32),
        scratch_shapes=[pltpu.VMEM((2 * tq, 2 * C_WIDTH), BF16)],
        name="band_attn",
        compiler_params=pltpu.CompilerParams(
            dimension_semantics=("arbitrary", "arbitrary"), vmem_limit_bytes=VMEM_LIMIT),
    )(q, kv, kv, bias)


def _out_kernel(alpha, pa_ref, gt_ref, yb_ref, yc_ref, x_ref, g1_ref, wout_ref, lng_ref, lnb_ref,
                sg_ref, sb_ref, wsp_ref, bsp_ref, seg2_ref, o_ref):
    pa = pa_ref[0]
    tm = pa.shape[0]
    u = pa[:, :A_WIDTH]
    v = pa[:, A_WIDTH:2 * A_WIDTH]
    ga = pa[:, 2 * A_WIDTH:]
    vc = v - _segsum(v, seg2_ref) * (1.0 / HEAD_DIM)
    var = _segsum(vc * vc, seg2_ref) * (1.0 / HEAD_DIM)
    vn = vc * lax.rsqrt(var + LN_EPS) * sg_ref[...] + sb_ref[...]

    first_head = lax.broadcasted_iota(jnp.int32, (SGU_BLOCK, LANES), 1) < HEAD_DIM
    prow = lax.broadcasted_iota(jnp.int32, (SGU_BLOCK, 2 * SGU_BLOCK), 0)
    pcol = lax.broadcasted_iota(jnp.int32, (SGU_BLOCK, 2 * SGU_BLOCK), 1) % SGU_BLOCK
    causal = prow >= pcol
    wsp = [jnp.where(causal, wsp_ref[gp], 0.0).astype(BF16) for gp in range(A_WIDTH // LANES)]
    blocks = []
    for n in range(tm // SGU_BLOCK):
        blk = vn[n * SGU_BLOCK:(n + 1) * SGU_BLOCK]
        cols = [_dot(wsp[gp], _stack_heads(blk[:, gp * LANES:(gp + 1) * LANES], first_head).astype(BF16))
                for gp in range(A_WIDTH // LANES)]
        blocks.append(jnp.concatenate(cols, axis=1) + bsp_ref[...])
    ya = _silu(ga) * u * jnp.concatenate(blocks, axis=0)

    gt = gt_ref[0]
    ybg = _silu(gt[:, :B_WIDTH]) * yb_ref[0]
    ycg = _silu(gt[:, B_WIDTH:]) * yc_ref[0]
    cat = jnp.concatenate([ya, ybg, ycg], axis=1).astype(BF16)
    z = alpha * x_ref[0] + g1_ref[0] * _dot(cat, wout_ref[...])
    zc = z - jnp.mean(z, axis=-1, keepdims=True)
    var = jnp.mean(zc * zc, axis=-1, keepdims=True)
    o_ref[0] = zc * lax.rsqrt(var + LN_EPS) * lng_ref[...] + lnb_ref[...]


def _output(alpha, pa, gates, yb, yc, x, gate1p, prm):
    bsz, t, _ = x.shape
    tm = OUT_TM
    row = lambda w: pl.BlockSpec((1, tm, w), lambda b, i: (b, i, 0))
    full = lambda a: pl.BlockSpec(a.shape, lambda b, i: (0,) * a.ndim)
    vec = lambda a: a.reshape(1, -1)
    w_sp = prm["w_spatial"]
    wsp = jnp.stack([jnp.concatenate([w_sp[2 * gp], w_sp[2 * gp + 1]], axis=1)
                     for gp in range(A_WIDTH // LANES)])
    bsp = jnp.repeat(prm["b_spatial"].T, HEAD_DIM, axis=1)
    head = jnp.arange(A_WIDTH) // HEAD_DIM
    seg = (head[:, None] == head[None, :]).astype(BF16)
    seg2 = jnp.concatenate([seg, seg], axis=0)
    consts = [prm["w_out"].astype(BF16), vec(prm["ln_g"]), vec(prm["ln_b"]), vec(prm["sgu_ln_g"]),
              vec(prm["sgu_ln_b"]), wsp, bsp, seg2]
    return pl.pallas_call(
        functools.partial(_out_kernel, alpha),
        grid=(bsz, t // tm),
        in_specs=[row(A_COLS), row(B_WIDTH + C_WIDTH), row(B_WIDTH), row(C_WIDTH), row(D_MODEL),
                  pl.BlockSpec((1, 1, D_MODEL), lambda b, i: (b, 0, 0))] + [full(a) for a in consts],
        out_specs=row(D_MODEL),
        out_shape=jax.ShapeDtypeStruct((bsz, t, D_MODEL), F32),
        name="sgu_out_norm",
        compiler_params=pltpu.CompilerParams(
            dimension_semantics=("arbitrary", "arbitrary"), vmem_limit_bytes=VMEM_LIMIT),
    )(pa, gates, yb, yc, x, gate1p, *consts)


def kernel(x, c, w_ada, b_ada, w_in, sgu_ln_g, sgu_ln_b, w_spatial, b_spatial, mu_shift, w_decay0, w_decay2, a0, a2, k_k, k_a, r_k, lnx_g, lnx_b, v0, v1, v2, rel_bias, w_out, ln_g, ln_b):
    depth = w_ada.shape[0]
    alpha = (2 * depth) ** 0.25
    bsz = x.shape[0]
    stacked = dict(sgu_ln_g=sgu_ln_g, sgu_ln_b=sgu_ln_b, w_spatial=w_spatial, b_spatial=b_spatial,
                   mu_shift=mu_shift, w_decay0=w_decay0, w_decay2=w_decay2, a0=a0, a2=a2, k_k=k_k, k_a=k_a,
                   r_k=r_k, lnx_g=lnx_g, lnx_b=lnx_b, w_out=w_out, ln_g=ln_g, ln_b=ln_b)
    v_first = None
    for i in range(depth):
        prm = {name: val[i] for name, val in stacked.items()}
        mod = _modulation(c, w_ada[i], b_ada[i]).reshape(bsz, 3, 1, D_MODEL)
        shift, scale1p, gate1p = mod[:, 0], 1.0 + mod[:, 1], 1.0 + mod[:, 2]
        pa, pbs, gates, q, kv = _projection(x, scale1p, shift, w_in[i])
        vres = (v0[i - 1], v1[i - 1], v2[i - 1]) if i > 0 else None
        yb, v_raw = _rwkv(pbs, v_first, prm, vres)
        if i == 0:
            v_first = v_raw
        yc = _attention(q, kv, rel_bias[i])
        x = _output(alpha, pa, gates, yb, yc, x, gate1p, prm)
    return x
```

```python
import functools

import jax
import jax.numpy as jnp
from jax import lax
from jax.experimental import pallas as pl
from jax.experimental.pallas import tpu as pltpu

F32 = jnp.float32
BF16 = jnp.bfloat16

D_MODEL = 1024
HEAD_DIM = 64
CHUNK = 64
SGU_BLOCK = 128
A_WIDTH = 256
B_WIDTH = 384
C_WIDTH = 384
LORA = 64
LEFT_CHUNKS = 8
BAND = (LEFT_CHUNKS + 1) * CHUNK
REL_CLIP = 256
A_COLS = 3 * A_WIDTH
B_SHIFT_COLS = 3 * B_WIDTH + 2 * LORA
B_COLS = B_SHIFT_COLS + B_WIDTH
C_COLS = 4 * C_WIDTH
PROJ_COLS = A_COLS + B_COLS + C_COLS
LN_EPS = 1e-5
GN_EPS = 64e-5

LANES = 128
PAIRS = B_WIDTH // LANES
VMEM_LIMIT = 56 * 1024 * 1024

PROJ_TM = 512
RWKV_CHUNKS = 4
ATTN_TQ = LEFT_CHUNKS * CHUNK
OUT_TM = 512

_NN = (((1,), (0,)), ((), ()))
_NT = (((1,), (1,)), ((), ()))
_TN = (((0,), (0,)), ((), ()))


def _dot(a, b, dims=_NN):
    return lax.dot_general(a, b, dims, preferred_element_type=F32)


def _split2(x):
    hi = x.astype(BF16)
    lo = (x - hi.astype(F32)).astype(BF16)
    return hi, lo


def _split3(x):
    hi = x.astype(BF16)
    rem = x - hi.astype(F32)
    mid = rem.astype(BF16)
    lo = (rem - mid.astype(F32)).astype(BF16)
    return hi, mid, lo


def _segsum(x, seg2_ref):
    hi, lo = _split2(x)
    return _dot(jnp.concatenate([hi, lo], axis=1), seg2_ref[...])


def _sigmoid(x):
    return 1.0 / (1.0 + jnp.exp(-x))


def _silu(x):
    return x * _sigmoid(x)


def _stack_heads(z, first_head):
    zero = jnp.zeros_like(z)
    return jnp.concatenate([jnp.where(first_head, z, zero), jnp.where(first_head, zero, z)], axis=0)


def _mod_kernel(c_ref, w_ref, b_ref, o_ref):
    c = c_ref[...]
    o_ref[...] = _dot(_silu(c).astype(BF16), w_ref[...]) + b_ref[...]


def _modulation(c, w_ada, b_ada):
    bsz = c.shape[0]
    return pl.pallas_call(
        _mod_kernel,
        out_shape=jax.ShapeDtypeStruct((bsz, 3 * D_MODEL), F32),
        name="adaln_mod",
        compiler_params=pltpu.CompilerParams(vmem_limit_bytes=VMEM_LIMIT),
    )(c, w_ada.astype(BF16), b_ada.reshape(1, -1))


def _proj_kernel(x_ref, sc_ref, sh_ref, w_ref, pa_ref, pbs_ref, gt_ref, q_ref, kv_ref):
    h = (x_ref[0] * sc_ref[0] + sh_ref[0]).astype(BF16)

    def seg(lo, hi):
        return _dot(h, w_ref[:, lo:hi])

    b0 = A_COLS
    c0 = A_COLS + B_COLS
    pa_ref[0] = seg(0, b0)
    pbs_ref[0] = seg(b0, b0 + B_SHIFT_COLS)
    gt_ref[0, :, :B_WIDTH] = seg(b0 + B_SHIFT_COLS, c0)
    gt_ref[0, :, B_WIDTH:] = seg(c0 + 3 * C_WIDTH, c0 + 4 * C_WIDTH)
    q_ref[0] = seg(c0, c0 + C_WIDTH).astype(BF16)
    kv_ref[0] = seg(c0 + C_WIDTH, c0 + 3 * C_WIDTH).astype(BF16)


def _projection(x, scale1p, shift, w_in):
    bsz, t, _ = x.shape
    tm = PROJ_TM
    row = lambda w: pl.BlockSpec((1, tm, w), lambda b, i: (b, i, 0))
    vec = pl.BlockSpec((1, 1, D_MODEL), lambda b, i: (b, 0, 0))
    return pl.pallas_call(
        _proj_kernel,
        grid=(bsz, t // tm),
        in_specs=[row(D_MODEL), vec, vec, pl.BlockSpec((D_MODEL, PROJ_COLS), lambda b, i: (0, 0))],
        out_specs=[row(A_COLS), row(B_SHIFT_COLS), row(B_WIDTH + C_WIDTH), row(C_WIDTH), row(2 * C_WIDTH)],
        out_shape=[
            jax.ShapeDtypeStruct((bsz, t, A_COLS), F32),
            jax.ShapeDtypeStruct((bsz, t, B_SHIFT_COLS), F32),
            jax.ShapeDtypeStruct((bsz, t, B_WIDTH + C_WIDTH), F32),
            jax.ShapeDtypeStruct((bsz, t, C_WIDTH), BF16),
            jax.ShapeDtypeStruct((bsz, t, 2 * C_WIDTH), BF16),
        ],
        name="in_proj",
        compiler_params=pltpu.CompilerParams(
            dimension_semantics=("arbitrary", "arbitrary"), vmem_limit_bytes=VMEM_LIMIT),
    )(x, scale1p, shift, w_in.astype(BF16))


def _rwkv_kernel(has_vres, nch, *refs):
    if has_vres:
        p_ref, vf_ref, v0_ref, v1_ref, v2_ref = refs[:5]
        refs = refs[5:]
    else:
        p_ref = refs[0]
        refs = refs[1:]
    (mu_ref, w0_ref, w2_ref, a0_ref, a2_ref, kk_ref, ka_ref, rk_ref, lng_ref, lnb_ref,
     seg2_ref, tri3_ref, gmask_ref, eye_ref, y_ref, vraw_ref, prev_ref, st_ref) = refs

    @pl.when(pl.program_id(1) == 0)
    def _():
        prev_ref[...] = jnp.zeros_like(prev_ref)
        st_ref[...] = jnp.zeros_like(st_ref)

    p = p_ref[0]
    tb = p.shape[0]
    row = lax.broadcasted_iota(jnp.int32, p.shape, 0)
    p_prev = jnp.where(row == 0, prev_ref[7:8, :], pltpu.roll(p, 1, 0))
    prev_ref[...] = p[tb - 8:tb, :]

    xs = p + (p_prev - p) * mu_ref[...]
    r = xs[:, 0:B_WIDTH]
    k = xs[:, B_WIDTH:2 * B_WIDTH]
    v = xs[:, 2 * B_WIDTH:3 * B_WIDTH]
    w_lo = xs[:, 3 * B_WIDTH:3 * B_WIDTH + LORA]
    a_lo = xs[:, 3 * B_WIDTH + LORA:]
    vraw_ref[0] = v

    zw = w0_ref[...] + _dot(jnp.tanh(w_lo).astype(BF16), w2_ref[...])
    softplus = jnp.maximum(-zw, 0.0) + jnp.log(1.0 + jnp.exp(-jnp.abs(zw)))
    lw = -jnp.exp(-softplus - 0.5)
    if has_vres:
        lora = _dot(_dot(v.astype(BF16), v1_ref[...]).astype(BF16), v2_ref[...])
        v = v + (vf_ref[0] - v) * _sigmoid(v0_ref[...] + lora)
    a = _sigmoid(a0_ref[...] + _dot(a_lo.astype(BF16), a2_ref[...]))
    kk = k * kk_ref[...]
    kk = kk * lax.rsqrt(jnp.maximum(_segsum(kk * kk, seg2_ref), 1e-24))
    k = k * (1.0 + (a - 1.0) * ka_ref[...])
    av = -kk
    bv = kk * a
    bonus = _segsum(r * k * rk_ref[...], seg2_ref) * v

    first_head = lax.broadcasted_iota(jnp.int32, (CHUNK, LANES), 1) < HEAD_DIM
    eye = eye_ref[...]
    gmask = gmask_ref[...]

    work = []
    for c in range(nch):
        rs = slice(c * CHUNK, (c + 1) * CHUNK)
        lw_c = lw[rs]
        cs = _dot(tri3_ref[...], jnp.concatenate(_split3(lw_c), axis=0))
        cs_end = cs[CHUNK - 1:CHUNK, :]
        at = av[rs] * jnp.exp(cs - lw_c)
        rt = r[rs] * jnp.exp(cs)
        inv = jnp.exp(-cs)
        bt = bv[rs] * inv
        kt = k[rs] * inv
        rem = jnp.exp(cs_end - cs)
        bp = bv[rs] * rem
        kp = k[rs] * rem
        p_end = jnp.exp(cs_end)
        vc = v[rs]
        for q in range(PAIRS):
            ls = slice(q * LANES, (q + 1) * LANES)
            w = dict(rs=rs, ls=ls, q=q, p_end=p_end[:, ls], rts=_stack_heads(rt[:, ls], first_head))
            for name, z in (("ats", at), ("bts", bt), ("kts", kt), ("bps", bp), ("kps", kp), ("vs", vc)):
                w[name] = _stack_heads(z[:, ls], first_head).astype(BF16)
            work.append(w)

    for w in work:
        g = _dot(jnp.concatenate([w["ats"], w["rts"].astype(BF16)], axis=0),
                 jnp.concatenate([w["bts"], w["kts"]], axis=0), _NT) * gmask
        w["gb"] = g.astype(BF16)
        w["tinv"] = eye + g[:LANES, :LANES]
    for w in work:
        a_ab = w["gb"][:LANES, :LANES]
        w["pw"] = _dot(a_ab, a_ab).astype(BF16)
    for level in range(1, 6):
        for w in work:
            if level < 5:
                both = _dot(jnp.concatenate([w["pw"], w["tinv"].astype(BF16)], axis=0), w["pw"])
                w["pw"] = both[:LANES].astype(BF16)
                w["tinv"] = w["tinv"] + both[LANES:]
            else:
                w["tinv"] = w["tinv"] + _dot(w["tinv"].astype(BF16), w["pw"])
    for w in work:
        w["avs"] = _dot(w["gb"][:LANES, LANES:], w["vs"]).astype(BF16)
    for w in work:
        wu = _dot(w["tinv"].astype(BF16), jnp.concatenate([w["ats"], w["avs"]], axis=1)).astype(BF16)
        w["w_s"] = wu[:, :LANES]
        w["uv"] = jnp.concatenate([wu[:, LANES:], w["vs"]], axis=0)
    for w in work:
        qp_s = w["rts"] + _dot(w["gb"][LANES:, :LANES], w["w_s"])
        w["qp"] = (qp_s[:CHUNK] + qp_s[CHUNK:]).astype(BF16)
    for w in work:
        y0_s = _dot(w["gb"][LANES:], w["uv"])
        w["y0"] = y0_s[:CHUNK] + y0_s[CHUNK:]
    for w in work:
        w["m"] = _dot(w["bps"], w["w_s"], _TN).astype(BF16)
    for w in work:
        w["n_t"] = _dot(w["uv"], jnp.concatenate([w["bps"], w["kps"]], axis=0), _TN)
    for w in work:
        s = st_ref[w["q"]]
        sb = s.astype(BF16)
        y_ref[0, w["rs"], w["ls"]] = _dot(w["qp"], sb, _NT) + w["y0"]
        st_ref[w["q"]] = s * w["p_end"] + _dot(sb, w["m"], _NT) + w["n_t"]

    y = y_ref[0]
    yc = y - _segsum(y, seg2_ref) * (1.0 / HEAD_DIM)
    var = _segsum(yc * yc, seg2_ref) * (1.0 / HEAD_DIM)
    y_ref[0] = yc * lax.rsqrt(var + GN_EPS) * lng_ref[...] + lnb_ref[...] + bonus


def _rwkv(pbs, v_first, prm, vres):
    bsz, t, _ = pbs.shape
    nch = RWKV_CHUNKS
    tb = nch * CHUNK
    has_vres = vres is not None
    row = lambda w: pl.BlockSpec((1, tb, w), lambda b, i: (b, i, 0))
    full = lambda a: pl.BlockSpec(a.shape, lambda b, i: (0,) * a.ndim)
    vec = lambda a: a.reshape(1, -1)

    head = jnp.arange(B_WIDTH) // HEAD_DIM
    seg = (head[:, None] == head[None, :]).astype(BF16)
    seg2 = jnp.concatenate([seg, seg], axis=0)
    tt = jnp.arange(CHUNK)
    tri = (tt[:, None] >= tt[None, :]).astype(BF16)
    tri3 = jnp.concatenate([tri, tri, tri], axis=1)
    idx = jnp.arange(2 * LANES)
    part, blk, pos = idx // LANES, (idx % LANES) // CHUNK, idx % CHUNK
    allowed = (blk[:, None] == blk[None, :]) & (
        (pos[None, :] < pos[:, None]) | ((part[:, None] == 1) & (pos[None, :] == pos[:, None])))
    gmask = allowed.astype(F32)
    eye = jnp.eye(LANES, dtype=F32)

    consts = [vec(prm["mu_shift"]), vec(prm["w_decay0"]), prm["w_decay2"].astype(BF16), vec(prm["a0"]),
              prm["a2"].astype(BF16), vec(prm["k_k"]), vec(prm["k_a"]), vec(prm["r_k"]), vec(prm["lnx_g"]),
              vec(prm["lnx_b"]), seg2, tri3, gmask, eye]
    args = [pbs]
    specs = [row(B_SHIFT_COLS)]
    if has_vres:
        v0, v1, v2 = vres
        extra = [vec(v0), v1.astype(BF16), v2.astype(BF16)]
        args += [v_first] + extra
        specs += [row(B_WIDTH)] + [full(a) for a in extra]
    args += consts
    specs += [full(a) for a in consts]
    return pl.pallas_call(
        functools.partial(_rwkv_kernel, has_vres, nch),
        grid=(bsz, t // tb),
        in_specs=specs,
        out_specs=[row(B_WIDTH), row(B_WIDTH)],
        out_shape=[jax.ShapeDtypeStruct((bsz, t, B_WIDTH), F32)] * 2,
        scratch_shapes=[pltpu.VMEM((8, B_SHIFT_COLS), F32), pltpu.VMEM((PAIRS, LANES, LANES), F32)],
        name="rwkv7_vres" if has_vres else "rwkv7",
        compiler_params=pltpu.CompilerParams(
            dimension_semantics=("arbitrary", "arbitrary"), vmem_limit_bytes=VMEM_LIMIT),
    )(*args)


def _attn_kernel(q_ref, kvp_ref, kvc_ref, bias_ref, o_ref, kv_ref):
    tq = ATTN_TQ
    kv_ref[0:tq] = kvp_ref[0]
    kv_ref[tq:2 * tq] = kvc_ref[0]
    not_first = pl.program_id(1) > 0
    first_head = lax.broadcasted_iota(jnp.int32, (CHUNK, LANES), 1) < HEAD_DIM
    col = lax.broadcasted_iota(jnp.int32, (2 * CHUNK, BAND), 1)
    scale = HEAD_DIM ** -0.5
    for c in range(tq // CHUNK):
        rs = slice(c * CHUNK, (c + 1) * CHUNK)
        band = slice(c * CHUNK, c * CHUNK + BAND)
        valid = (col >= tq - c * CHUNK) | not_first
        for pr in range(PAIRS):
            ls = slice(pr * LANES, (pr + 1) * LANES)
            qs = _stack_heads(q_ref[0, rs, ls], first_head)
            kb = kv_ref[band, ls]
            vb = kv_ref[band, C_WIDTH + pr * LANES:C_WIDTH + (pr + 1) * LANES]
            s = _dot(qs, kb, _NT) * scale + bias_ref[pr]
            s = jnp.where(valid, s, -jnp.inf)
            e = jnp.exp(s - jnp.max(s, axis=-1, keepdims=True))
            o = _dot(e.astype(BF16), vb) / jnp.sum(e, axis=-1, keepdims=True)
            o_ref[0, rs, ls] = jnp.where(first_head, o[:CHUNK], o[CHUNK:])


def _attention(q, kv, rel_table):
    bsz, t, _ = q.shape
    tq = ATTN_TQ
    pad = LEFT_CHUNKS * CHUNK
    dist = jnp.arange(CHUNK)[:, None] + pad - jnp.arange(BAND)[None, :]
    idx = jnp.clip(dist, -REL_CLIP, REL_CLIP) + REL_CLIP
    bias = rel_table[:, idx].astype(F32).reshape(PAIRS, 2 * CHUNK, BAND)
    return pl.pallas_call(
        _attn_kernel,
        grid=(bsz, t // tq),
        in_specs=[
            pl.BlockSpec((1, tq, C_WIDTH), lambda b, i: (b, i, 0)),
            pl.BlockSpec((1, tq, 2 * C_WIDTH), lambda b, i: (b, jnp.maximum(i - 1, 0), 0)),
            pl.BlockSpec((1, tq, 2 * C_WIDTH), lambda b, i: (b, i, 0)),
            pl.BlockSpec(bias.shape, lambda b, i: (0, 0, 0)),
        ],
        out_specs=pl.BlockSpec((1, tq, C_WIDTH), lambda b, i: (b, i, 0)),
        out_shape=jax.ShapeDtypeStruct((bsz, t, C_WIDTH), F32),
        scratch_shapes=[pltpu.VMEM((2 * tq, 2 * C_WIDTH), BF16)],
        name="band_attn",
        compiler_params=pltpu.CompilerParams(
            dimension_semantics=("arbitrary", "arbitrary"), vmem_limit_bytes=VMEM_LIMIT),
    )(q, kv, kv, bias)


def _out_kernel(alpha, pa_ref, gt_ref, yb_ref, yc_ref, x_ref, g1_ref, wout_ref, lng_ref, lnb_ref,
                sg_ref, sb_ref, wsp_ref, bsp_ref, seg2_ref, o_ref):
    pa = pa_ref[0]
    tm = pa.shape[0]
    u = pa[:, :A_WIDTH]
    v = pa[:, A_WIDTH:2 * A_WIDTH]
    ga = pa[:, 2 * A_WIDTH:]
    vc = v - _segsum(v, seg2_ref) * (1.0 / HEAD_DIM)
    var = _segsum(vc * vc, seg2_ref) * (1.0 / HEAD_DIM)
    vn = vc * lax.rsqrt(var + LN_EPS) * sg_ref[...] + sb_ref[...]

    first_head = lax.broadcasted_iota(jnp.int32, (SGU_BLOCK, LANES), 1) < HEAD_DIM
    prow = lax.broadcasted_iota(jnp.int32, (SGU_BLOCK, 2 * SGU_BLOCK), 0)
    pcol = lax.broadcasted_iota(jnp.int32, (SGU_BLOCK, 2 * SGU_BLOCK), 1) % SGU_BLOCK
    causal = prow >= pcol
    wsp = [jnp.where(causal, wsp_ref[gp], 0.0).astype(BF16) for gp in range(A_WIDTH // LANES)]
    blocks = []
    for n in range(tm // SGU_BLOCK):
        blk = vn[n * SGU_BLOCK:(n + 1) * SGU_BLOCK]
        cols = [_dot(wsp[gp], _stack_heads(blk[:, gp * LANES:(gp + 1) * LANES], first_head).astype(BF16))
                for gp in range(A_WIDTH // LANES)]
        blocks.append(jnp.concatenate(cols, axis=1) + bsp_ref[...])
    ya = _silu(ga) * u * jnp.concatenate(blocks, axis=0)

    gt = gt_ref[0]
    ybg = _silu(gt[:, :B_WIDTH]) * yb_ref[0]
    ycg = _silu(gt[:, B_WIDTH:]) * yc_ref[0]
    cat = jnp.concatenate([ya, ybg, ycg], axis=1).astype(BF16)
    z = alpha * x_ref[0] + g1_ref[0] * _dot(cat, wout_ref[...])
    zc = z - jnp.mean(z, axis=-1, keepdims=True)
    var = jnp.mean(zc * zc, axis=-1, keepdims=True)
    o_ref[0] = zc * lax.rsqrt(var + LN_EPS) * lng_ref[...] + lnb_ref[...]


def _output(alpha, pa, gates, yb, yc, x, gate1p, prm):
    bsz, t, _ = x.shape
    tm = OUT_TM
    row = lambda w: pl.BlockSpec((1, tm, w), lambda b, i: (b, i, 0))
    full = lambda a: pl.BlockSpec(a.shape, lambda b, i: (0,) * a.ndim)
    vec = lambda a: a.reshape(1, -1)
    w_sp = prm["w_spatial"]
    wsp = jnp.stack([jnp.concatenate([w_sp[2 * gp], w_sp[2 * gp + 1]], axis=1)
                     for gp in range(A_WIDTH // LANES)])
    bsp = jnp.repeat(prm["b_spatial"].T, HEAD_DIM, axis=1)
    head = jnp.arange(A_WIDTH) // HEAD_DIM
    seg = (head[:, None] == head[None, :]).astype(BF16)
    seg2 = jnp.concatenate([seg, seg], axis=0)
    consts = [prm["w_out"].astype(BF16), vec(prm["ln_g"]), vec(prm["ln_b"]), vec(prm["sgu_ln_g"]),
              vec(prm["sgu_ln_b"]), wsp, bsp, seg2]
    return pl.pallas_call(
        functools.partial(_out_kernel, alpha),
        grid=(bsz, t // tm),
        in_specs=[row(A_COLS), row(B_WIDTH + C_WIDTH), row(B_WIDTH), row(C_WIDTH), row(D_MODEL),
                  pl.BlockSpec((1, 1, D_MODEL), lambda b, i: (b, 0, 0))] + [full(a) for a in consts],
        out_specs=row(D_MODEL),
        out_shape=jax.ShapeDtypeStruct((bsz, t, D_MODEL), F32),
        name="sgu_out_norm",
        compiler_params=pltpu.CompilerParams(
            dimension_semantics=("arbitrary", "arbitrary"), vmem_limit_bytes=VMEM_LIMIT),
    )(pa, gates, yb, yc, x, gate1p, *consts)


def kernel(x, c, w_ada, b_ada, w_in, sgu_ln_g, sgu_ln_b, w_spatial, b_spatial, mu_shift, w_decay0, w_decay2, a0, a2, k_k, k_a, r_k, lnx_g, lnx_b, v0, v1, v2, rel_bias, w_out, ln_g, ln_b):
    depth = w_ada.shape[0]
    alpha = (2 * depth) ** 0.25
    bsz = x.shape[0]
    stacked = dict(sgu_ln_g=sgu_ln_g, sgu_ln_b=sgu_ln_b, w_spatial=w_spatial, b_spatial=b_spatial,
                   mu_shift=mu_shift, w_decay0=w_decay0, w_decay2=w_decay2, a0=a0, a2=a2, k_k=k_k, k_a=k_a,
                   r_k=r_k, lnx_g=lnx_g, lnx_b=lnx_b, w_out=w_out, ln_g=ln_g, ln_b=ln_b)
    v_first = None
    for i in range(depth):
        prm = {name: val[i] for name, val in stacked.items()}
        mod = _modulation(c, w_ada[i], b_ada[i]).reshape(bsz, 3, 1, D_MODEL)
        shift, scale1p, gate1p = mod[:, 0], 1.0 + mod[:, 1], 1.0 + mod[:, 2]
        pa, pbs, gates, q, kv = _projection(x, scale1p, shift, w_in[i])
        vres = (v0[i - 1], v1[i - 1], v2[i - 1]) if i > 0 else None
        yb, v_raw = _rwkv(pbs, v_first, prm, vres)
        if i == 0:
            v_first = v_raw
        yc = _attention(q, kv, rel_bias[i])
        x = _output(alpha, pa, gates, yb, yc, x, gate1p, prm)
    return x
```

```python
import functools

import numpy as np

import jax
import jax.numpy as jnp
from jax import lax
from jax.experimental import pallas as pl
from jax.experimental.pallas import tpu as pltpu

F32 = jnp.float32
BF16 = jnp.bfloat16

D_MODEL = 1024
HEAD_DIM = 64
CHUNK = 64
SGU_BLOCK = 128
A_WIDTH = 256
B_WIDTH = 384
C_WIDTH = 384
C_HEADS = C_WIDTH // HEAD_DIM
LORA = 64
LEFT_CHUNKS = 8
BAND = (LEFT_CHUNKS + 1) * CHUNK
REL_CLIP = 256
A_COLS = 3 * A_WIDTH
B_SHIFT_COLS = 3 * B_WIDTH + 2 * LORA
B_COLS = B_SHIFT_COLS + B_WIDTH
C_COLS = 4 * C_WIDTH
PROJ_COLS = A_COLS + B_COLS + C_COLS
LN_EPS = 1e-5
GN_EPS = 64e-5

LANES = 128
PAIRS = B_WIDTH // LANES
VMEM_LIMIT = 56 * 1024 * 1024

PROJ_TM = 512
RWKV_CHUNKS = 4
ATTN_TQ = LEFT_CHUNKS * CHUNK
OUT_TM = 512
BIAS_EXT = -(-(BAND + CHUNK) // LANES) * LANES

_NN = (((1,), (0,)), ((), ()))
_NT = (((1,), (1,)), ((), ()))
_TN = (((0,), (0,)), ((), ()))


def _dot(a, b, dims=_NN):
    return lax.dot_general(a, b, dims, preferred_element_type=F32)


def _split2(x):
    hi = x.astype(BF16)
    lo = (x - hi.astype(F32)).astype(BF16)
    return hi, lo


def _split3(x):
    hi = x.astype(BF16)
    rem = x - hi.astype(F32)
    mid = rem.astype(BF16)
    lo = (rem - mid.astype(F32)).astype(BF16)
    return hi, mid, lo


def _segsum(x, seg2_ref):
    outs = []
    for q in range(x.shape[1] // LANES):
        hi, lo = _split2(x[:, q * LANES:(q + 1) * LANES])
        outs.append(_dot(jnp.concatenate([hi, lo], axis=1), seg2_ref[...]))
    return jnp.concatenate(outs, axis=1)


def _sigmoid(x):
    return 1.0 / (1.0 + jnp.exp(-x))


def _silu(x):
    return x * _sigmoid(x)


def _stack_heads(z, first_head):
    zero = jnp.zeros_like(z)
    return jnp.concatenate([jnp.where(first_head, z, zero), jnp.where(first_head, zero, z)], axis=0)


def _pair_seg2():
    head = jnp.arange(LANES) // HEAD_DIM
    seg = (head[:, None] == head[None, :]).astype(BF16)
    return jnp.concatenate([seg, seg], axis=0)


_row = lambda rows, w: pl.BlockSpec((1, rows, w), lambda b, i: (b, i, 0))
_full = lambda a: pl.BlockSpec(a.shape, lambda b, i: (0,) * a.ndim)
_vec = lambda a: a.reshape(1, -1)
_GRID_PARAMS = pltpu.CompilerParams(
    dimension_semantics=("arbitrary", "arbitrary"), vmem_limit_bytes=VMEM_LIMIT)


def _mod_kernel(c_ref, w_ref, b_ref, o_ref):
    c = c_ref[...]
    o_ref[...] = _dot(_silu(c).astype(BF16), w_ref[...]) + b_ref[...]


def _modulation(c, w_ada, b_ada):
    bsz = c.shape[0]
    return pl.pallas_call(
        _mod_kernel,
        out_shape=jax.ShapeDtypeStruct((bsz, 3 * D_MODEL), F32),
        name="adaln_mod",
        compiler_params=pltpu.CompilerParams(vmem_limit_bytes=VMEM_LIMIT),
    )(c, w_ada.astype(BF16), b_ada.reshape(1, -1))


def _proj_kernel(x_ref, sc_ref, sh_ref, w_ref, sg_ref, sb_ref, wsp_ref, bsp_ref, seg2_ref,
                 ya_ref, pbs_ref, gb_ref, gc_ref, q_ref, kv_ref):
    h = (x_ref[0] * sc_ref[0] + sh_ref[0]).astype(BF16)

    def seg(lo, hi):
        return _dot(h, w_ref[:, lo:hi])

    b0 = A_COLS
    c0 = A_COLS + B_COLS
    pbs_ref[0] = seg(b0, b0 + B_SHIFT_COLS)
    gb_ref[0] = seg(b0 + B_SHIFT_COLS, c0)
    q_ref[0] = seg(c0, c0 + C_WIDTH).astype(BF16)
    kv_ref[0] = seg(c0 + C_WIDTH, c0 + 3 * C_WIDTH).astype(BF16)
    gc_ref[0] = seg(c0 + 3 * C_WIDTH, c0 + 4 * C_WIDTH)

    u = seg(0, A_WIDTH)
    v = seg(A_WIDTH, 2 * A_WIDTH)
    ga = seg(2 * A_WIDTH, 3 * A_WIDTH)
    tm = v.shape[0]
    vc = v - _segsum(v, seg2_ref) * (1.0 / HEAD_DIM)
    var = _segsum(vc * vc, seg2_ref) * (1.0 / HEAD_DIM)
    vn = vc * lax.rsqrt(var + LN_EPS) * sg_ref[...] + sb_ref[...]

    first_head = lax.broadcasted_iota(jnp.int32, (SGU_BLOCK, LANES), 1) < HEAD_DIM
    prow = lax.broadcasted_iota(jnp.int32, (SGU_BLOCK, 2 * SGU_BLOCK), 0)
    pcol = lax.broadcasted_iota(jnp.int32, (SGU_BLOCK, 2 * SGU_BLOCK), 1) % SGU_BLOCK
    causal = prow >= pcol
    wsp = [jnp.where(causal, wsp_ref[gp], 0.0).astype(BF16) for gp in range(A_WIDTH // LANES)]
    blocks = []
    for n in range(tm // SGU_BLOCK):
        blk = vn[n * SGU_BLOCK:(n + 1) * SGU_BLOCK]
        cols = [_dot(wsp[gp], _stack_heads(blk[:, gp * LANES:(gp + 1) * LANES], first_head).astype(BF16))
                for gp in range(A_WIDTH // LANES)]
        blocks.append(jnp.concatenate(cols, axis=1) + bsp_ref[...])
    ya_ref[0] = (_silu(ga) * u * jnp.concatenate(blocks, axis=0)).astype(BF16)


def _projection(x, scale1p, shift, w_in, prm):
    bsz, t, _ = x.shape
    tm = PROJ_TM
    vec = pl.BlockSpec((1, 1, D_MODEL), lambda b, i: (b, 0, 0))
    w_sp = prm["w_spatial"]
    wsp = jnp.stack([jnp.concatenate([w_sp[2 * gp], w_sp[2 * gp + 1]], axis=1)
                     for gp in range(A_WIDTH // LANES)])
    bsp = jnp.repeat(prm["b_spatial"].T, HEAD_DIM, axis=1)
    consts = [w_in.astype(BF16), _vec(prm["sgu_ln_g"]), _vec(prm["sgu_ln_b"]), wsp, bsp, _pair_seg2()]
    widths = [(A_WIDTH, BF16), (B_SHIFT_COLS, F32), (B_WIDTH, F32), (C_WIDTH, F32),
              (C_WIDTH, BF16), (2 * C_WIDTH, BF16)]
    return pl.pallas_call(
        _proj_kernel,
        grid=(bsz, t // tm),
        in_specs=[_row(tm, D_MODEL), vec, vec] + [_full(a) for a in consts],
        out_specs=[_row(tm, w) for w, _ in widths],
        out_shape=[jax.ShapeDtypeStruct((bsz, t, w), dt) for w, dt in widths],
        name="in_proj_sgu",
        compiler_params=_GRID_PARAMS,
    )(x, scale1p, shift, *consts)


def _rwkv_kernel(has_vres, nch, *refs):
    if has_vres:
        p_ref, gate_ref, vf_ref, v0_ref, v1_ref, v2_ref = refs[:6]
        refs = refs[6:]
    else:
        p_ref, gate_ref = refs[:2]
        refs = refs[2:]
    (mu_ref, w0_ref, w2_ref, a0_ref, a2_ref, kk_ref, ka_ref, rk_ref, lng_ref, lnb_ref,
     seg2_ref, tri3_ref, gmask_ref, eye_ref) = refs[:14]
    refs = refs[14:]
    if has_vres:
        y_ref, yn_ref, prev_ref, st_ref = refs
    else:
        y_ref, vraw_ref, yn_ref, prev_ref, st_ref = refs

    @pl.when(pl.program_id(1) == 0)
    def _():
        prev_ref[...] = jnp.zeros_like(prev_ref)
        st_ref[...] = jnp.zeros_like(st_ref)

    p = p_ref[0]
    tb = p.shape[0]
    row = lax.broadcasted_iota(jnp.int32, p.shape, 0)
    p_prev = jnp.where(row == 0, prev_ref[7:8, :], pltpu.roll(p, 1, 0))
    prev_ref[...] = p[tb - 8:tb, :]

    xs = p + (p_prev - p) * mu_ref[...]
    r = xs[:, 0:B_WIDTH]
    k = xs[:, B_WIDTH:2 * B_WIDTH]
    v = xs[:, 2 * B_WIDTH:3 * B_WIDTH]
    w_lo = xs[:, 3 * B_WIDTH:3 * B_WIDTH + LORA]
    a_lo = xs[:, 3 * B_WIDTH + LORA:]
    if not has_vres:
        vraw_ref[0] = v

    zw = w0_ref[...] + _dot(jnp.tanh(w_lo).astype(BF16), w2_ref[...])
    softplus = jnp.maximum(-zw, 0.0) + jnp.log(1.0 + jnp.exp(-jnp.abs(zw)))
    lw = -jnp.exp(-softplus - 0.5)
    if has_vres:
        lora = _dot(_dot(v.astype(BF16), v1_ref[...]).astype(BF16), v2_ref[...])
        v = v + (vf_ref[0] - v) * _sigmoid(v0_ref[...] + lora)
    a = _sigmoid(a0_ref[...] + _dot(a_lo.astype(BF16), a2_ref[...]))
    kk = k * kk_ref[...]
    kk = kk * lax.rsqrt(jnp.maximum(_segsum(kk * kk, seg2_ref), 1e-24))
    k = k * (1.0 + (a - 1.0) * ka_ref[...])
    av = -kk
    bv = kk * a
    bonus = _segsum(r * k * rk_ref[...], seg2_ref) * v

    first_head = lax.broadcasted_iota(jnp.int32, (CHUNK, LANES), 1) < HEAD_DIM
    eye = eye_ref[...]
    gmask = gmask_ref[...]

    work = []
    for c in range(nch):
        rs = slice(c * CHUNK, (c + 1) * CHUNK)
        lw_c = lw[rs]
        cs = _dot(tri3_ref[...], jnp.concatenate(_split3(lw_c), axis=0))
        cs_end = cs[CHUNK - 1:CHUNK, :]
        at = av[rs] * jnp.exp(cs - lw_c)
        rt = r[rs] * jnp.exp(cs)
        inv = jnp.exp(-cs)
        bt = bv[rs] * inv
        kt = k[rs] * inv
        rem = jnp.exp(cs_end - cs)
        bp = bv[rs] * rem
        kp = k[rs] * rem
        p_end = jnp.exp(cs_end)
        vc = v[rs]
        for q in range(PAIRS):
            ls = slice(q * LANES, (q + 1) * LANES)
            w = dict(rs=rs, ls=ls, q=q, p_end=p_end[:, ls], rts=_stack_heads(rt[:, ls], first_head))
            for name, z in (("ats", at), ("bts", bt), ("kts", kt), ("bps", bp), ("kps", kp), ("vs", vc)):
                w[name] = _stack_heads(z[:, ls], first_head).astype(BF16)
            work.append(w)

    for w in work:
        g = _dot(jnp.concatenate([w["ats"], w["rts"].astype(BF16)], axis=0),
                 jnp.concatenate([w["bts"], w["kts"]], axis=0), _NT) * gmask
        w["gb"] = g.astype(BF16)
        w["tinv"] = eye + g[:LANES, :LANES]
    for w in work:
        a_ab = w["gb"][:LANES, :LANES]
        w["pw"] = _dot(a_ab, a_ab).astype(BF16)
    for level in range(1, 6):
        for w in work:
            if level < 5:
                both = _dot(jnp.concatenate([w["pw"], w["tinv"].astype(BF16)], axis=0), w["pw"])
                w["pw"] = both[:LANES].astype(BF16)
                w["tinv"] = w["tinv"] + both[LANES:]
            else:
                w["tinv"] = w["tinv"] + _dot(w["tinv"].astype(BF16), w["pw"])
    for w in work:
        w["avs"] = _dot(w["gb"][:LANES, LANES:], w["vs"]).astype(BF16)
    for w in work:
        wu = _dot(w["tinv"].astype(BF16), jnp.concatenate([w["ats"], w["avs"]], axis=1)).astype(BF16)
        w["w_s"] = wu[:, :LANES]
        w["uv"] = jnp.concatenate([wu[:, LANES:], w["vs"]], axis=0)
    for w in work:
        qp_s = w["rts"] + _dot(w["gb"][LANES:, :LANES], w["w_s"])
        w["qp"] = (qp_s[:CHUNK] + qp_s[CHUNK:]).astype(BF16)
    for w in work:
        y0_s = _dot(w["gb"][LANES:], w["uv"])
        w["y0"] = y0_s[:CHUNK] + y0_s[CHUNK:]
    for w in work:
        w["m"] = _dot(w["bps"], w["w_s"], _TN).astype(BF16)
    for w in work:
        w["n_t"] = _dot(w["uv"], jnp.concatenate([w["bps"], w["kps"]], axis=0), _TN)
    for w in work:
        s = st_ref[w["q"]]
        sb = s.astype(BF16)
        yn_ref[w["rs"], w["ls"]] = _dot(w["qp"], sb, _NT) + w["y0"]
        st_ref[w["q"]] = s * w["p_end"] + _dot(sb, w["m"], _NT) + w["n_t"]

    y = yn_ref[...]
    yc = y - _segsum(y, seg2_ref) * (1.0 / HEAD_DIM)
    var = _segsum(yc * yc, seg2_ref) * (1.0 / HEAD_DIM)
    y = yc * lax.rsqrt(var + GN_EPS) * lng_ref[...] + lnb_ref[...] + bonus
    y_ref[0] = (_silu(gate_ref[0]) * y).astype(BF16)


def _rwkv(pbs, gate, v_first, prm, vres):
    bsz, t, _ = pbs.shape
    nch = RWKV_CHUNKS
    tb = nch * CHUNK
    has_vres = vres is not None

    tt = jnp.arange(CHUNK)
    tri = (tt[:, None] >= tt[None, :]).astype(BF16)
    tri3 = jnp.concatenate([tri, tri, tri], axis=1)
    idx = jnp.arange(2 * LANES)
    part, blk, pos = idx // LANES, (idx % LANES) // CHUNK, idx % CHUNK
    allowed = (blk[:, None] == blk[None, :]) & (
        (pos[None, :] < pos[:, None]) | ((part[:, None] == 1) & (pos[None, :] == pos[:, None])))
    gmask = allowed.astype(F32)
    eye = jnp.eye(LANES, dtype=F32)

    consts = [_vec(prm["mu_shift"]), _vec(prm["w_decay0"]), prm["w_decay2"].astype(BF16), _vec(prm["a0"]),
              prm["a2"].astype(BF16), _vec(prm["k_k"]), _vec(prm["k_a"]), _vec(prm["r_k"]), _vec(prm["lnx_g"]),
              _vec(prm["lnx_b"]), _pair_seg2(), tri3, gmask, eye]
    args = [pbs, gate]
    specs = [_row(tb, B_SHIFT_COLS), _row(tb, B_WIDTH)]
    out_specs = [_row(tb, B_WIDTH)]
    out_shape = [jax.ShapeDtypeStruct((bsz, t, B_WIDTH), BF16)]
    if has_vres:
        v0, v1, v2 = vres
        extra = [_vec(v0), v1.astype(BF16), v2.astype(BF16)]
        args += [v_first] + extra
        specs += [_row(tb, B_WIDTH)] + [_full(a) for a in extra]
    else:
        out_specs.append(_row(tb, B_WIDTH))
        out_shape.append(jax.ShapeDtypeStruct((bsz, t, B_WIDTH), F32))
    args += consts
    specs += [_full(a) for a in consts]
    return pl.pallas_call(
        functools.partial(_rwkv_kernel, has_vres, nch),
        grid=(bsz, t // tb),
        in_specs=specs,
        out_specs=out_specs,
        out_shape=out_shape,
        scratch_shapes=[pltpu.VMEM((tb, B_WIDTH), F32), pltpu.VMEM((8, B_SHIFT_COLS), F32),
                        pltpu.VMEM((PAIRS, LANES, LANES), F32)],
        name="rwkv7_vres" if has_vres else "rwkv7",
        compiler_params=_GRID_PARAMS,
    )(*args)


def _attn_kernel(q_ref, kvp_ref, kvc_ref, gate_ref, e_ref, o_ref, kv_ref, bias_ref):
    tq = ATTN_TQ

    @pl.when((pl.program_id(0) == 0) & (pl.program_id(1) == 0))
    def _():
        for h in range(C_HEADS):
            rows = jnp.broadcast_to(e_ref[h:h + 1, :], (CHUNK, BIAS_EXT))
            tile = pltpu.roll(rows, 0, 1, stride=1, stride_axis=0)
            bias_ref[h // 2, (h % 2) * CHUNK:(h % 2 + 1) * CHUNK, :] = tile[:, :BAND]

    kv_ref[0:tq] = kvp_ref[0]
    kv_ref[tq:2 * tq] = kvc_ref[0]
    not_first = pl.program_id(1) > 0
    first_head = lax.broadcasted_iota(jnp.int32, (CHUNK, LANES), 1) < HEAD_DIM
    col = lax.broadcasted_iota(jnp.int32, (2 * CHUNK, BAND), 1)
    scale = HEAD_DIM ** -0.5

    def scores(c):
        rs = slice(c * CHUNK, (c + 1) * CHUNK)
        band = slice(c * CHUNK, c * CHUNK + BAND)
        valid = (col >= tq - c * CHUNK) | not_first
        out = []
        for pr in range(PAIRS):
            ls = slice(pr * LANES, (pr + 1) * LANES)
            qs = _stack_heads(q_ref[0, rs, ls] * scale, first_head)
            s = _dot(qs, kv_ref[band, ls], _NT) + bias_ref[pr]
            out.append(jnp.where(valid, s, -jnp.inf))
        return out

    def finish(c, ss):
        rs = slice(c * CHUNK, (c + 1) * CHUNK)
        band = slice(c * CHUNK, c * CHUNK + BAND)
        for pr, s in enumerate(ss):
            ls = slice(pr * LANES, (pr + 1) * LANES)
            vb = kv_ref[band, C_WIDTH + pr * LANES:C_WIDTH + (pr + 1) * LANES]
            e = jnp.exp(s - jnp.max(s, axis=-1, keepdims=True))
            o = _dot(e.astype(BF16), vb) / jnp.sum(e, axis=-1, keepdims=True)
            y = jnp.where(first_head, o[:CHUNK], o[CHUNK:])
            o_ref[0, rs, ls] = (_silu(gate_ref[0, rs, ls]) * y).astype(BF16)

    pending = scores(0)
    for c in range(1, tq // CHUNK):
        nxt = scores(c)
        finish(c - 1, pending)
        pending = nxt
    finish(tq // CHUNK - 1, pending)


def _attention(q, kv, gate, rel_table):
    bsz, t, _ = q.shape
    tq = ATTN_TQ
    pad = LEFT_CHUNKS * CHUNK
    m = np.arange(BIAS_EXT)
    dist = np.where(m < BAND, pad - m, pad + BIAS_EXT - m)
    e = rel_table[:, np.clip(dist, -REL_CLIP, REL_CLIP) + REL_CLIP].astype(F32)
    return pl.pallas_call(
        _attn_kernel,
        grid=(bsz, t // tq),
        in_specs=[
            _row(tq, C_WIDTH),
            pl.BlockSpec((1, tq, 2 * C_WIDTH), lambda b, i: (b, jnp.maximum(i - 1, 0), 0)),
            _row(tq, 2 * C_WIDTH),
            _row(tq, C_WIDTH),
            _full(e),
        ],
        out_specs=_row(tq, C_WIDTH),
        out_shape=jax.ShapeDtypeStruct((bsz, t, C_WIDTH), BF16),
        scratch_shapes=[pltpu.VMEM((2 * tq, 2 * C_WIDTH), BF16),
                        pltpu.VMEM((PAIRS, 2 * CHUNK, BAND), F32)],
        name="band_attn",
        compiler_params=_GRID_PARAMS,
    )(q, kv, kv, gate, e)


def _out_kernel(alpha, ya_ref, yb_ref, yc_ref, x_ref, g1_ref, wout_ref, lng_ref, lnb_ref, o_ref):
    cat = jnp.concatenate([ya_ref[0], yb_ref[0], yc_ref[0]], axis=1)
    z = alpha * x_ref[0] + g1_ref[0] * _dot(cat, wout_ref[...])
    zc = z - jnp.mean(z, axis=-1, keepdims=True)
    var = jnp.mean(zc * zc, axis=-1, keepdims=True)
    o_ref[0] = zc * lax.rsqrt(var + LN_EPS) * lng_ref[...] + lnb_ref[...]


def _output(alpha, ya, yb, yc, x, gate1p, prm):
    bsz, t, _ = x.shape
    tm = OUT_TM
    consts = [prm["w_out"].astype(BF16), _vec(prm["ln_g"]), _vec(prm["ln_b"])]
    return pl.pallas_call(
        functools.partial(_out_kernel, alpha),
        grid=(bsz, t // tm),
        in_specs=[_row(tm, A_WIDTH), _row(tm, B_WIDTH), _row(tm, C_WIDTH), _row(tm, D_MODEL),
                  pl.BlockSpec((1, 1, D_MODEL), lambda b, i: (b, 0, 0))] + [_full(a) for a in consts],
        out_specs=_row(tm, D_MODEL),
        out_shape=jax.ShapeDtypeStruct((bsz, t, D_MODEL), F32),
        name="out_proj_norm",
        compiler_params=_GRID_PARAMS,
    )(ya, yb, yc, x, gate1p, *consts)


def kernel(x, c, w_ada, b_ada, w_in, sgu_ln_g, sgu_ln_b, w_spatial, b_spatial, mu_shift, w_decay0, w_decay2, a0, a2, k_k, k_a, r_k, lnx_g, lnx_b, v0, v1, v2, rel_bias, w_out, ln_g, ln_b):
    depth = w_ada.shape[0]
    alpha = (2 * depth) ** 0.25
    bsz = x.shape[0]
    stacked = dict(sgu_ln_g=sgu_ln_g, sgu_ln_b=sgu_ln_b, w_spatial=w_spatial, b_spatial=b_spatial,
                   mu_shift=mu_shift, w_decay0=w_decay0, w_decay2=w_decay2, a0=a0, a2=a2, k_k=k_k, k_a=k_a,
                   r_k=r_k, lnx_g=lnx_g, lnx_b=lnx_b, w_out=w_out, ln_g=ln_g, ln_b=ln_b)
    v_first = None
    for i in range(depth):
        prm = {name: val[i] for name, val in stacked.items()}
        mod = _modulation(c, w_ada[i], b_ada[i]).reshape(bsz, 3, 1, D_MODEL)
        shift, scale1p, gate1p = mod[:, 0], 1.0 + mod[:, 1], 1.0 + mod[:, 2]
        ya, pbs, gate_b, gate_c, q, kv = _projection(x, scale1p, shift, w_in[i], prm)
        if i == 0:
            yb, v_first = _rwkv(pbs, gate_b, None, prm, None)
        else:
            (yb,) = _rwkv(pbs, gate_b, v_first, prm, (v0[i - 1], v1[i - 1], v2[i - 1]))
        yc = _attention(q, kv, gate_c, rel_bias[i])
        x = _output(alpha, ya, yb, yc, x, gate1p, prm)
    return x
```

```python
import functools

import numpy as np

import jax
import jax.numpy as jnp
from jax import lax
from jax.experimental import pallas as pl
from jax.experimental.pallas import tpu as pltpu

F32 = jnp.float32
BF16 = jnp.bfloat16

D_MODEL = 1024
HEAD_DIM = 64
CHUNK = 64
SGU_BLOCK = 128
A_WIDTH = 256
B_WIDTH = 384
C_WIDTH = 384
C_HEADS = C_WIDTH // HEAD_DIM
LORA = 64
LEFT_CHUNKS = 8
BAND = (LEFT_CHUNKS + 1) * CHUNK
REL_CLIP = 256
A_COLS = 3 * A_WIDTH
B_SHIFT_COLS = 3 * B_WIDTH + 2 * LORA
B_COLS = B_SHIFT_COLS + B_WIDTH
C_COLS = 4 * C_WIDTH
PROJ_COLS = A_COLS + B_COLS + C_COLS
LN_EPS = 1e-5
GN_EPS = 64e-5

LANES = 128
PAIRS = B_WIDTH // LANES
VMEM_LIMIT = 56 * 1024 * 1024

PROJ_TM = 512
RWKV_CHUNKS = 8
ATTN_TQ = LEFT_CHUNKS * CHUNK
OUT_TM = 1024
BIAS_EXT = -(-(BAND + CHUNK) // LANES) * LANES

_NN = (((1,), (0,)), ((), ()))
_NT = (((1,), (1,)), ((), ()))
_TN = (((0,), (0,)), ((), ()))


def _dot(a, b, dims=_NN):
    return lax.dot_general(a, b, dims, preferred_element_type=F32)


def _split2(x):
    hi = x.astype(BF16)
    lo = (x - hi.astype(F32)).astype(BF16)
    return hi, lo


def _split3(x):
    hi = x.astype(BF16)
    rem = x - hi.astype(F32)
    mid = rem.astype(BF16)
    lo = (rem - mid.astype(F32)).astype(BF16)
    return hi, mid, lo


def _segsum(x, seg2_ref):
    outs = []
    for q in range(x.shape[1] // LANES):
        hi, lo = _split2(x[:, q * LANES:(q + 1) * LANES])
        outs.append(_dot(jnp.concatenate([hi, lo], axis=1), seg2_ref[...]))
    return jnp.concatenate(outs, axis=1)


def _sigmoid(x):
    return 1.0 / (1.0 + jnp.exp(-x))


def _silu(x):
    return x * _sigmoid(x)


def _stack_heads(z, first_head):
    zero = jnp.zeros_like(z)
    return jnp.concatenate([jnp.where(first_head, z, zero), jnp.where(first_head, zero, z)], axis=0)


def _pair_seg2():
    head = jnp.arange(LANES) // HEAD_DIM
    seg = (head[:, None] == head[None, :]).astype(BF16)
    return jnp.concatenate([seg, seg], axis=0)


_row = lambda rows, w: pl.BlockSpec((1, rows, w), lambda b, i: (b, i, 0))
_full = lambda a: pl.BlockSpec(a.shape, lambda b, i: (0,) * a.ndim)
_vec = lambda a: a.reshape(1, -1)
_GRID_PARAMS = pltpu.CompilerParams(
    dimension_semantics=("arbitrary", "arbitrary"), vmem_limit_bytes=VMEM_LIMIT)


def _mod_kernel(c_ref, w_ref, b_ref, o_ref):
    c = c_ref[...]
    o_ref[...] = _dot(_silu(c).astype(BF16), w_ref[...]) + b_ref[...]


def _modulation(c, w_ada, b_ada):
    bsz = c.shape[0]
    return pl.pallas_call(
        _mod_kernel,
        out_shape=jax.ShapeDtypeStruct((bsz, 3 * D_MODEL), F32),
        name="adaln_mod",
        compiler_params=pltpu.CompilerParams(vmem_limit_bytes=VMEM_LIMIT),
    )(c, w_ada.astype(BF16), b_ada.reshape(1, -1))


def _proj_kernel(x_ref, sc_ref, sh_ref, w_ref, sg_ref, sb_ref, wsp_ref, bsp_ref, seg2_ref,
                 ya_ref, pbs_ref, gb_ref, gc_ref, q_ref, kv_ref):
    h = (x_ref[0] * sc_ref[0] + sh_ref[0]).astype(BF16)

    def seg(lo, hi):
        return _dot(h, w_ref[:, lo:hi])

    b0 = A_COLS
    c0 = A_COLS + B_COLS
    pbs_ref[0] = seg(b0, b0 + B_SHIFT_COLS)
    gb_ref[0] = seg(b0 + B_SHIFT_COLS, c0)
    q_ref[0] = seg(c0, c0 + C_WIDTH).astype(BF16)
    kv_ref[0] = seg(c0 + C_WIDTH, c0 + 3 * C_WIDTH).astype(BF16)
    gc_ref[0] = seg(c0 + 3 * C_WIDTH, c0 + 4 * C_WIDTH)

    u = seg(0, A_WIDTH)
    v = seg(A_WIDTH, 2 * A_WIDTH)
    ga = seg(2 * A_WIDTH, 3 * A_WIDTH)
    tm = v.shape[0]
    vc = v - _segsum(v, seg2_ref) * (1.0 / HEAD_DIM)
    var = _segsum(vc * vc, seg2_ref) * (1.0 / HEAD_DIM)
    vn = vc * lax.rsqrt(var + LN_EPS) * sg_ref[...] + sb_ref[...]

    first_head = lax.broadcasted_iota(jnp.int32, (SGU_BLOCK, LANES), 1) < HEAD_DIM
    prow = lax.broadcasted_iota(jnp.int32, (SGU_BLOCK, 2 * SGU_BLOCK), 0)
    pcol = lax.broadcasted_iota(jnp.int32, (SGU_BLOCK, 2 * SGU_BLOCK), 1) % SGU_BLOCK
    causal = prow >= pcol
    wsp = [jnp.where(causal, wsp_ref[gp], 0.0).astype(BF16) for gp in range(A_WIDTH // LANES)]
    blocks = []
    for n in range(tm // SGU_BLOCK):
        blk = vn[n * SGU_BLOCK:(n + 1) * SGU_BLOCK]
        cols = [_dot(wsp[gp], _stack_heads(blk[:, gp * LANES:(gp + 1) * LANES], first_head).astype(BF16))
                for gp in range(A_WIDTH // LANES)]
        blocks.append(jnp.concatenate(cols, axis=1) + bsp_ref[...])
    ya_ref[0] = (_silu(ga) * u * jnp.concatenate(blocks, axis=0)).astype(BF16)


def _projection(x, scale1p, shift, w_in, prm):
    bsz, t, _ = x.shape
    tm = PROJ_TM
    vec = pl.BlockSpec((1, 1, D_MODEL), lambda b, i: (b, 0, 0))
    w_sp = prm["w_spatial"]
    wsp = jnp.stack([jnp.concatenate([w_sp[2 * gp], w_sp[2 * gp + 1]], axis=1)
                     for gp in range(A_WIDTH // LANES)])
    bsp = jnp.repeat(prm["b_spatial"].T, HEAD_DIM, axis=1)
    consts = [w_in.astype(BF16), _vec(prm["sgu_ln_g"]), _vec(prm["sgu_ln_b"]), wsp, bsp, _pair_seg2()]
    widths = [(A_WIDTH, BF16), (B_SHIFT_COLS, F32), (B_WIDTH, F32), (C_WIDTH, F32),
              (C_WIDTH, BF16), (2 * C_WIDTH, BF16)]
    return pl.pallas_call(
        _proj_kernel,
        grid=(bsz, t // tm),
        in_specs=[_row(tm, D_MODEL), vec, vec] + [_full(a) for a in consts],
        out_specs=[_row(tm, w) for w, _ in widths],
        out_shape=[jax.ShapeDtypeStruct((bsz, t, w), dt) for w, dt in widths],
        name="in_proj_sgu",
        compiler_params=_GRID_PARAMS,
    )(x, scale1p, shift, *consts)


def _rwkv_kernel(has_vres, nch, *refs):
    if has_vres:
        p_ref, gate_ref, vf_ref, v0_ref, v1_ref, v2_ref = refs[:6]
        refs = refs[6:]
    else:
        p_ref, gate_ref = refs[:2]
        refs = refs[2:]
    (mu_ref, w0_ref, w2_ref, a0_ref, a2_ref, kk_ref, ka_ref, rk_ref, lng_ref, lnb_ref,
     seg2_ref, tri3_ref, gmask_ref, eye_ref) = refs[:14]
    refs = refs[14:]
    if has_vres:
        y_ref, yn_ref, prev_ref, st_ref = refs
    else:
        y_ref, vraw_ref, yn_ref, prev_ref, st_ref = refs

    @pl.when(pl.program_id(1) == 0)
    def _():
        prev_ref[...] = jnp.zeros_like(prev_ref)
        st_ref[...] = jnp.zeros_like(st_ref)

    p = p_ref[0]
    tb = p.shape[0]
    row = lax.broadcasted_iota(jnp.int32, p.shape, 0)
    p_prev = jnp.where(row == 0, prev_ref[7:8, :], pltpu.roll(p, 1, 0))
    prev_ref[...] = p[tb - 8:tb, :]

    xs = p + (p_prev - p) * mu_ref[...]
    r = xs[:, 0:B_WIDTH]
    k = xs[:, B_WIDTH:2 * B_WIDTH]
    v = xs[:, 2 * B_WIDTH:3 * B_WIDTH]
    w_lo = xs[:, 3 * B_WIDTH:3 * B_WIDTH + LORA]
    a_lo = xs[:, 3 * B_WIDTH + LORA:]
    if not has_vres:
        vraw_ref[0] = v

    zw = w0_ref[...] + _dot(jnp.tanh(w_lo).astype(BF16), w2_ref[...])
    softplus = jnp.maximum(-zw, 0.0) + jnp.log(1.0 + jnp.exp(-jnp.abs(zw)))
    lw = -jnp.exp(-softplus - 0.5)
    if has_vres:
        lora = _dot(_dot(v.astype(BF16), v1_ref[...]).astype(BF16), v2_ref[...])
        v = v + (vf_ref[0] - v) * _sigmoid(v0_ref[...] + lora)
    a = _sigmoid(a0_ref[...] + _dot(a_lo.astype(BF16), a2_ref[...]))
    kk = k * kk_ref[...]
    kk = kk * lax.rsqrt(jnp.maximum(_segsum(kk * kk, seg2_ref), 1e-24))
    k = k * (1.0 + (a - 1.0) * ka_ref[...])
    av = -kk
    bv = kk * a
    bonus = _segsum(r * k * rk_ref[...], seg2_ref) * v

    first_head = lax.broadcasted_iota(jnp.int32, (CHUNK, LANES), 1) < HEAD_DIM
    eye = eye_ref[...]
    gmask = gmask_ref[...]

    work = []
    for c in range(nch):
        rs = slice(c * CHUNK, (c + 1) * CHUNK)
        lw_c = lw[rs]
        cs = _dot(tri3_ref[...], jnp.concatenate(_split3(lw_c), axis=0))
        cs_end = cs[CHUNK - 1:CHUNK, :]
        at = av[rs] * jnp.exp(cs - lw_c)
        rt = r[rs] * jnp.exp(cs)
        inv = jnp.exp(-cs)
        bt = bv[rs] * inv
        kt = k[rs] * inv
        rem = jnp.exp(cs_end - cs)
        bp = bv[rs] * rem
        kp = k[rs] * rem
        p_end = jnp.exp(cs_end)
        vc = v[rs]
        for q in range(PAIRS):
            ls = slice(q * LANES, (q + 1) * LANES)
            w = dict(rs=rs, ls=ls, q=q, p_end=p_end[:, ls], rts=_stack_heads(rt[:, ls], first_head))
            for name, z in (("ats", at), ("bts", bt), ("kts", kt), ("bps", bp), ("kps", kp), ("vs", vc)):
                w[name] = _stack_heads(z[:, ls].astype(BF16), first_head)
            work.append(w)

    for w in work:
        g = _dot(jnp.concatenate([w["ats"], w["rts"].astype(BF16)], axis=0),
                 jnp.concatenate([w["bts"], w["kts"]], axis=0), _NT) * gmask
        w["gb"] = g.astype(BF16)
        w["tinv"] = eye + g[:LANES, :LANES]
    for w in work:
        a_ab = w["gb"][:LANES, :LANES]
        w["pw"] = _dot(a_ab, a_ab).astype(BF16)
    for level in range(1, 6):
        for w in work:
            if level < 5:
                both = _dot(jnp.concatenate([w["pw"], w["tinv"].astype(BF16)], axis=0), w["pw"])
                w["pw"] = both[:LANES].astype(BF16)
                w["tinv"] = w["tinv"] + both[LANES:]
            else:
                w["tinv"] = w["tinv"] + _dot(w["tinv"].astype(BF16), w["pw"])
    for w in work:
        w["avs"] = _dot(w["gb"][:LANES, LANES:], w["vs"]).astype(BF16)
    for w in work:
        wu = _dot(w["tinv"].astype(BF16), jnp.concatenate([w["ats"], w["avs"]], axis=1)).astype(BF16)
        w["w_s"] = wu[:, :LANES]
        w["uv"] = jnp.concatenate([wu[:, LANES:], w["vs"]], axis=0)
    for w in work:
        qp_s = w["rts"] + _dot(w["gb"][LANES:, :LANES], w["w_s"])
        w["qp"] = (qp_s[:CHUNK] + qp_s[CHUNK:]).astype(BF16)
    for w in work:
        y0_s = _dot(w["gb"][LANES:], w["uv"])
        w["y0"] = y0_s[:CHUNK] + y0_s[CHUNK:]
    for w in work:
        w["m"] = _dot(w["bps"], w["w_s"], _TN).astype(BF16)
    for w in work:
        w["n_t"] = _dot(w["uv"], jnp.concatenate([w["bps"], w["kps"]], axis=0), _TN)
    for w in work:
        s = st_ref[w["q"]]
        sb = s.astype(BF16)
        yn_ref[w["rs"], w["ls"]] = _dot(w["qp"], sb, _NT) + w["y0"]
        st_ref[w["q"]] = s * w["p_end"] + _dot(sb, w["m"], _NT) + w["n_t"]

    y = yn_ref[...]
    yc = y - _segsum(y, seg2_ref) * (1.0 / HEAD_DIM)
    var = _segsum(yc * yc, seg2_ref) * (1.0 / HEAD_DIM)
    y = yc * lax.rsqrt(var + GN_EPS) * lng_ref[...] + lnb_ref[...] + bonus
    y_ref[0] = (_silu(gate_ref[0]) * y).astype(BF16)


def _rwkv(pbs, gate, v_first, prm, vres):
    bsz, t, _ = pbs.shape
    nch = RWKV_CHUNKS
    tb = nch * CHUNK
    has_vres = vres is not None

    tt = jnp.arange(CHUNK)
    tri = (tt[:, None] >= tt[None, :]).astype(BF16)
    tri3 = jnp.concatenate([tri, tri, tri], axis=1)
    idx = jnp.arange(2 * LANES)
    part, blk, pos = idx // LANES, (idx % LANES) // CHUNK, idx % CHUNK
    allowed = (blk[:, None] == blk[None, :]) & (
        (pos[None, :] < pos[:, None]) | ((part[:, None] == 1) & (pos[None, :] == pos[:, None])))
    gmask = allowed.astype(F32)
    eye = jnp.eye(LANES, dtype=F32)

    consts = [_vec(prm["mu_shift"]), _vec(prm["w_decay0"]), prm["w_decay2"].astype(BF16), _vec(prm["a0"]),
              prm["a2"].astype(BF16), _vec(prm["k_k"]), _vec(prm["k_a"]), _vec(prm["r_k"]), _vec(prm["lnx_g"]),
              _vec(prm["lnx_b"]), _pair_seg2(), tri3, gmask, eye]
    args = [pbs, gate]
    specs = [_row(tb, B_SHIFT_COLS), _row(tb, B_WIDTH)]
    out_specs = [_row(tb, B_WIDTH)]
    out_shape = [jax.ShapeDtypeStruct((bsz, t, B_WIDTH), BF16)]
    if has_vres:
        v0, v1, v2 = vres
        extra = [_vec(v0), v1.astype(BF16), v2.astype(BF16)]
        args += [v_first] + extra
        specs += [_row(tb, B_WIDTH)] + [_full(a) for a in extra]
    else:
        out_specs.append(_row(tb, B_WIDTH))
        out_shape.append(jax.ShapeDtypeStruct((bsz, t, B_WIDTH), F32))
    args += consts
    specs += [_full(a) for a in consts]
    return pl.pallas_call(
        functools.partial(_rwkv_kernel, has_vres, nch),
        grid=(bsz, t // tb),
        in_specs=specs,
        out_specs=out_specs,
        out_shape=out_shape,
        scratch_shapes=[pltpu.VMEM((tb, B_WIDTH), F32), pltpu.VMEM((8, B_SHIFT_COLS), F32),
                        pltpu.VMEM((PAIRS, LANES, LANES), F32)],
        name="rwkv7_vres" if has_vres else "rwkv7",
        compiler_params=_GRID_PARAMS,
    )(*args)


def _attn_kernel(q_ref, kvp_ref, kvc_ref, gate_ref, e_ref, o_ref, kv_ref, bias_ref):
    tq = ATTN_TQ

    @pl.when((pl.program_id(0) == 0) & (pl.program_id(1) == 0))
    def _():
        key = lax.broadcasted_iota(jnp.int32, (CHUNK, BAND), 1)
        for h in range(C_HEADS):
            rows = jnp.broadcast_to(e_ref[h:h + 1, :], (CHUNK, BIAS_EXT))
            tile = pltpu.roll(rows, 0, 1, stride=1, stride_axis=0)[:, :BAND]
            hs = slice((h % 2) * CHUNK, (h % 2 + 1) * CHUNK)
            bias_ref[0, h // 2, hs, :] = tile
            for c in range(tq // CHUNK):
                bias_ref[1 + c, h // 2, hs, :] = jnp.where(key >= tq - c * CHUNK, tile, -jnp.inf)

    kv_ref[0:tq] = kvp_ref[0]
    kv_ref[tq:2 * tq] = kvc_ref[0]
    first_block = pl.program_id(1) == 0
    first_head = lax.broadcasted_iota(jnp.int32, (CHUNK, LANES), 1) < HEAD_DIM
    scale = HEAD_DIM ** -0.5

    def scores(c):
        rs = slice(c * CHUNK, (c + 1) * CHUNK)
        band = slice(c * CHUNK, c * CHUNK + BAND)
        table = jnp.where(first_block, 1 + c, 0)
        out = []
        for pr in range(PAIRS):
            ls = slice(pr * LANES, (pr + 1) * LANES)
            qs = _stack_heads(q_ref[0, rs, ls] * scale, first_head)
            out.append(_dot(qs, kv_ref[band, ls], _NT) + bias_ref[table, pr])
        return out

    def finish(c, ss):
        rs = slice(c * CHUNK, (c + 1) * CHUNK)
        band = slice(c * CHUNK, c * CHUNK + BAND)
        for pr, s in enumerate(ss):
            ls = slice(pr * LANES, (pr + 1) * LANES)
            vb = kv_ref[band, C_WIDTH + pr * LANES:C_WIDTH + (pr + 1) * LANES]
            e = jnp.exp(s - jnp.max(s, axis=-1, keepdims=True))
            o = _dot(e.astype(BF16), vb) / jnp.sum(e, axis=-1, keepdims=True)
            y = jnp.where(first_head, o[:CHUNK], o[CHUNK:])
            o_ref[0, rs, ls] = (_silu(gate_ref[0, rs, ls]) * y).astype(BF16)

    pending = scores(0)
    for c in range(1, tq // CHUNK):
        nxt = scores(c)
        finish(c - 1, pending)
        pending = nxt
    finish(tq // CHUNK - 1, pending)


def _attention(q, kv, gate, rel_table):
    bsz, t, _ = q.shape
    tq = ATTN_TQ
    pad = LEFT_CHUNKS * CHUNK
    m = np.arange(BIAS_EXT)
    dist = np.where(m < BAND, pad - m, pad + BIAS_EXT - m)
    e = rel_table[:, np.clip(dist, -REL_CLIP, REL_CLIP) + REL_CLIP].astype(F32)
    return pl.pallas_call(
        _attn_kernel,
        grid=(bsz, t // tq),
        in_specs=[
            _row(tq, C_WIDTH),
            pl.BlockSpec((1, tq, 2 * C_WIDTH), lambda b, i: (b, jnp.maximum(i - 1, 0), 0)),
            _row(tq, 2 * C_WIDTH),
            _row(tq, C_WIDTH),
            _full(e),
        ],
        out_specs=_row(tq, C_WIDTH),
        out_shape=jax.ShapeDtypeStruct((bsz, t, C_WIDTH), BF16),
        scratch_shapes=[pltpu.VMEM((2 * tq, 2 * C_WIDTH), BF16),
                        pltpu.VMEM((1 + tq // CHUNK, PAIRS, 2 * CHUNK, BAND), F32)],
        name="band_attn",
        compiler_params=_GRID_PARAMS,
    )(q, kv, kv, gate, e)


def _out_kernel(alpha, ya_ref, yb_ref, yc_ref, x_ref, g1_ref, wout_ref, lng_ref, lnb_ref, o_ref):
    cat = jnp.concatenate([ya_ref[0], yb_ref[0], yc_ref[0]], axis=1)
    z = alpha * x_ref[0] + g1_ref[0] * _dot(cat, wout_ref[...])
    zc = z - jnp.mean(z, axis=-1, keepdims=True)
    var = jnp.mean(zc * zc, axis=-1, keepdims=True)
    o_ref[0] = zc * lax.rsqrt(var + LN_EPS) * lng_ref[...] + lnb_ref[...]


def _output(alpha, ya, yb, yc, x, gate1p, prm):
    bsz, t, _ = x.shape
    tm = OUT_TM
    consts = [prm["w_out"].astype(BF16), _vec(prm["ln_g"]), _vec(prm["ln_b"])]
    return pl.pallas_call(
        functools.partial(_out_kernel, alpha),
        grid=(bsz, t // tm),
        in_specs=[_row(tm, A_WIDTH), _row(tm, B_WIDTH), _row(tm, C_WIDTH), _row(tm, D_MODEL),
                  pl.BlockSpec((1, 1, D_MODEL), lambda b, i: (b, 0, 0))] + [_full(a) for a in consts],
        out_specs=_row(tm, D_MODEL),
        out_shape=jax.ShapeDtypeStruct((bsz, t, D_MODEL), F32),
        name="out_proj_norm",
        compiler_params=_GRID_PARAMS,
    )(ya, yb, yc, x, gate1p, *consts)


def kernel(x, c, w_ada, b_ada, w_in, sgu_ln_g, sgu_ln_b, w_spatial, b_spatial, mu_shift, w_decay0, w_decay2, a0, a2, k_k, k_a, r_k, lnx_g, lnx_b, v0, v1, v2, rel_bias, w_out, ln_g, ln_b):
    depth = w_ada.shape[0]
    alpha = (2 * depth) ** 0.25
    bsz = x.shape[0]
    stacked = dict(sgu_ln_g=sgu_ln_g, sgu_ln_b=sgu_ln_b, w_spatial=w_spatial, b_spatial=b_spatial,
                   mu_shift=mu_shift, w_decay0=w_decay0, w_decay2=w_decay2, a0=a0, a2=a2, k_k=k_k, k_a=k_a,
                   r_k=r_k, lnx_g=lnx_g, lnx_b=lnx_b, w_out=w_out, ln_g=ln_g, ln_b=ln_b)
    v_first = None
    for i in range(depth):
        prm = {name: val[i] for name, val in stacked.items()}
        mod = _modulation(c, w_ada[i], b_ada[i]).reshape(bsz, 3, 1, D_MODEL)
        shift, scale1p, gate1p = mod[:, 0], 1.0 + mod[:, 1], 1.0 + mod[:, 2]
        ya, pbs, gate_b, gate_c, q, kv = _projection(x, scale1p, shift, w_in[i], prm)
        if i == 0:
            yb, v_first = _rwkv(pbs, gate_b, None, prm, None)
        else:
            (yb,) = _rwkv(pbs, gate_b, v_first, prm, (v0[i - 1], v1[i - 1], v2[i - 1]))
        yc = _attention(q, kv, gate_c, rel_bias[i])
        x = _output(alpha, ya, yb, yc, x, gate1p, prm)
    return x
```

```python
import functools

import numpy as np

import jax
import jax.numpy as jnp
from jax import lax
from jax.experimental import pallas as pl
from jax.experimental.pallas import tpu as pltpu

F32 = jnp.float32
BF16 = jnp.bfloat16

D_MODEL = 1024
HEAD_DIM = 64
CHUNK = 64
SGU_BLOCK = 128
A_WIDTH = 256
B_WIDTH = 384
C_WIDTH = 384
C_HEADS = C_WIDTH // HEAD_DIM
LORA = 64
LEFT_CHUNKS = 8
BAND = (LEFT_CHUNKS + 1) * CHUNK
REL_CLIP = 256
A_COLS = 3 * A_WIDTH
B_SHIFT_COLS = 3 * B_WIDTH + 2 * LORA
B_COLS = B_SHIFT_COLS + B_WIDTH
C_COLS = 4 * C_WIDTH
PROJ_COLS = A_COLS + B_COLS + C_COLS
LN_EPS = 1e-5
GN_EPS = 64e-5

LANES = 128
PAIRS = B_WIDTH // LANES
VMEM_LIMIT = 56 * 1024 * 1024

PROJ_TM = 512
RWKV_CHUNKS = 8
ATTN_TQ = LEFT_CHUNKS * CHUNK
OUT_TM = 1024
BIAS_EXT = -(-(BAND + CHUNK) // LANES) * LANES

_NN = (((1,), (0,)), ((), ()))
_NT = (((1,), (1,)), ((), ()))
_TN = (((0,), (0,)), ((), ()))


def _dot(a, b, dims=_NN):
    return lax.dot_general(a, b, dims, preferred_element_type=F32)


def _split2(x):
    hi = x.astype(BF16)
    lo = (x - hi.astype(F32)).astype(BF16)
    return hi, lo


def _split3(x):
    hi = x.astype(BF16)
    rem = x - hi.astype(F32)
    mid = rem.astype(BF16)
    lo = (rem - mid.astype(F32)).astype(BF16)
    return hi, mid, lo


def _segsum(x, seg2_ref):
    outs = []
    for q in range(x.shape[1] // LANES):
        hi, lo = _split2(x[:, q * LANES:(q + 1) * LANES])
        outs.append(_dot(jnp.concatenate([hi, lo], axis=1), seg2_ref[...]))
    return jnp.concatenate(outs, axis=1)


def _sigmoid(x):
    return 1.0 / (1.0 + jnp.exp(-x))


def _silu(x):
    return x * _sigmoid(x)


def _stack_heads(z, first_head):
    zero = jnp.zeros_like(z)
    return jnp.concatenate([jnp.where(first_head, z, zero), jnp.where(first_head, zero, z)], axis=0)


def _pair_seg2():
    head = jnp.arange(LANES) // HEAD_DIM
    seg = (head[:, None] == head[None, :]).astype(BF16)
    return jnp.concatenate([seg, seg], axis=0)


_row = lambda rows, w: pl.BlockSpec((1, rows, w), lambda b, i: (b, i, 0))
_full = lambda a: pl.BlockSpec(a.shape, lambda b, i: (0,) * a.ndim)
_vec = lambda a: a.reshape(1, -1)
_GRID_PARAMS = pltpu.CompilerParams(
    dimension_semantics=("arbitrary", "arbitrary"), vmem_limit_bytes=VMEM_LIMIT)


def _mod_kernel(c_ref, w_ref, b_ref, o_ref):
    c = c_ref[...]
    o_ref[...] = _dot(_silu(c).astype(BF16), w_ref[...]) + b_ref[...]


def _modulation(c, w_ada, b_ada):
    bsz = c.shape[0]
    return pl.pallas_call(
        _mod_kernel,
        out_shape=jax.ShapeDtypeStruct((bsz, 3 * D_MODEL), F32),
        name="adaln_mod",
        compiler_params=pltpu.CompilerParams(vmem_limit_bytes=VMEM_LIMIT),
    )(c, w_ada.astype(BF16), b_ada.reshape(1, -1))


def _proj_kernel(x_ref, sc_ref, sh_ref, w_ref, sg_ref, sb_ref, wsp_ref, bsp_ref, seg2_ref,
                 ya_ref, pbs_ref, gb_ref, gc_ref, q_ref, kv_ref):
    h = (x_ref[0] * sc_ref[0] + sh_ref[0]).astype(BF16)

    edges = np.cumsum([0, A_COLS, B_SHIFT_COLS, B_WIDTH + C_WIDTH, 3 * C_WIDTH])
    pa, pbs, gates, qkv = (_dot(h, w_ref[:, lo:hi]) for lo, hi in zip(edges[:-1], edges[1:]))
    pbs_ref[0] = pbs
    gb_ref[0] = gates[:, :B_WIDTH]
    gc_ref[0] = gates[:, B_WIDTH:]
    q_ref[0] = qkv[:, :C_WIDTH].astype(BF16)
    kv_ref[0] = qkv[:, C_WIDTH:].astype(BF16)

    u = pa[:, :A_WIDTH]
    v = pa[:, A_WIDTH:2 * A_WIDTH]
    ga = pa[:, 2 * A_WIDTH:]
    tm = v.shape[0]
    vc = v - _segsum(v, seg2_ref) * (1.0 / HEAD_DIM)
    var = _segsum(vc * vc, seg2_ref) * (1.0 / HEAD_DIM)
    vn = vc * lax.rsqrt(var + LN_EPS) * sg_ref[...] + sb_ref[...]

    first_head = lax.broadcasted_iota(jnp.int32, (SGU_BLOCK, LANES), 1) < HEAD_DIM
    prow = lax.broadcasted_iota(jnp.int32, (SGU_BLOCK, 2 * SGU_BLOCK), 0)
    pcol = lax.broadcasted_iota(jnp.int32, (SGU_BLOCK, 2 * SGU_BLOCK), 1) % SGU_BLOCK
    causal = prow >= pcol
    wsp = [jnp.where(causal, wsp_ref[gp], 0.0).astype(BF16) for gp in range(A_WIDTH // LANES)]
    blocks = []
    for n in range(tm // SGU_BLOCK):
        blk = vn[n * SGU_BLOCK:(n + 1) * SGU_BLOCK]
        cols = [_dot(wsp[gp], _stack_heads(blk[:, gp * LANES:(gp + 1) * LANES], first_head).astype(BF16))
                for gp in range(A_WIDTH // LANES)]
        blocks.append(jnp.concatenate(cols, axis=1) + bsp_ref[...])
    ya_ref[0] = (_silu(ga) * u * jnp.concatenate(blocks, axis=0)).astype(BF16)


def _projection(x, scale1p, shift, w_in, prm):
    bsz, t, _ = x.shape
    tm = PROJ_TM
    vec = pl.BlockSpec((1, 1, D_MODEL), lambda b, i: (b, 0, 0))
    w_sp = prm["w_spatial"]
    wsp = jnp.stack([jnp.concatenate([w_sp[2 * gp], w_sp[2 * gp + 1]], axis=1)
                     for gp in range(A_WIDTH // LANES)])
    bsp = jnp.repeat(prm["b_spatial"].T, HEAD_DIM, axis=1)
    c0 = A_COLS + B_COLS
    w_cols = jnp.concatenate(
        [w_in[:, :c0], w_in[:, c0 + 3 * C_WIDTH:], w_in[:, c0:c0 + 3 * C_WIDTH]], axis=1).astype(BF16)
    consts = [w_cols, _vec(prm["sgu_ln_g"]), _vec(prm["sgu_ln_b"]), wsp, bsp, _pair_seg2()]
    widths = [(A_WIDTH, BF16), (B_SHIFT_COLS, F32), (B_WIDTH, F32), (C_WIDTH, F32),
              (C_WIDTH, BF16), (2 * C_WIDTH, BF16)]
    return pl.pallas_call(
        _proj_kernel,
        grid=(bsz, t // tm),
        in_specs=[_row(tm, D_MODEL), vec, vec] + [_full(a) for a in consts],
        out_specs=[_row(tm, w) for w, _ in widths],
        out_shape=[jax.ShapeDtypeStruct((bsz, t, w), dt) for w, dt in widths],
        name="in_proj_sgu",
        compiler_params=_GRID_PARAMS,
    )(x, scale1p, shift, *consts)


def _rwkv_kernel(has_vres, nch, *refs):
    if has_vres:
        p_ref, gate_ref, vf_ref, v0_ref, v1_ref, v2_ref = refs[:6]
        refs = refs[6:]
    else:
        p_ref, gate_ref = refs[:2]
        refs = refs[2:]
    (mu_ref, w0_ref, w2_ref, a0_ref, a2_ref, kk_ref, ka_ref, rk_ref, lng_ref, lnb_ref,
     seg2_ref, tri3_ref, gmask_ref, eye_ref) = refs[:14]
    refs = refs[14:]
    if has_vres:
        y_ref, yn_ref, prev_ref, st_ref = refs
    else:
        y_ref, vraw_ref, yn_ref, prev_ref, st_ref = refs

    @pl.when(pl.program_id(1) == 0)
    def _():
        prev_ref[...] = jnp.zeros_like(prev_ref)
        st_ref[...] = jnp.zeros_like(st_ref)

    p = p_ref[0]
    tb = p.shape[0]
    row = lax.broadcasted_iota(jnp.int32, p.shape, 0)
    p_prev = jnp.where(row == 0, prev_ref[7:8, :], pltpu.roll(p, 1, 0))
    prev_ref[...] = p[tb - 8:tb, :]

    xs = p + (p_prev - p) * mu_ref[...]
    r = xs[:, 0:B_WIDTH]
    k = xs[:, B_WIDTH:2 * B_WIDTH]
    v = xs[:, 2 * B_WIDTH:3 * B_WIDTH]
    w_lo = xs[:, 3 * B_WIDTH:3 * B_WIDTH + LORA]
    a_lo = xs[:, 3 * B_WIDTH + LORA:]
    if not has_vres:
        vraw_ref[0] = v

    zw = w0_ref[...] + _dot(jnp.tanh(w_lo).astype(BF16), w2_ref[...])
    lw = _sigmoid(zw) * (-np.exp(-0.5).astype(np.float32))
    if has_vres:
        lora = _dot(_dot(v.astype(BF16), v1_ref[...]).astype(BF16), v2_ref[...])
        v = v + (vf_ref[0] - v) * _sigmoid(v0_ref[...] + lora)
    a = _sigmoid(a0_ref[...] + _dot(a_lo.astype(BF16), a2_ref[...]))
    kk = k * kk_ref[...]
    kk = kk * lax.rsqrt(jnp.maximum(_segsum(kk * kk, seg2_ref), 1e-24))
    k = k * (1.0 + (a - 1.0) * ka_ref[...])
    av = -kk
    bv = kk * a
    bonus = _segsum(r * k * rk_ref[...], seg2_ref) * v

    first_head = lax.broadcasted_iota(jnp.int32, (CHUNK, LANES), 1) < HEAD_DIM
    eye = eye_ref[...]
    gmask = gmask_ref[...]

    work = []
    for c in range(nch):
        rs = slice(c * CHUNK, (c + 1) * CHUNK)
        lw_c = lw[rs]
        cs = _dot(tri3_ref[...], jnp.concatenate(_split3(lw_c), axis=0))
        cs_end = cs[CHUNK - 1:CHUNK, :]
        at = av[rs] * jnp.exp(cs - lw_c)
        rt = r[rs] * jnp.exp(cs)
        inv = jnp.exp(-cs)
        bt = bv[rs] * inv
        kt = k[rs] * inv
        rem = jnp.exp(cs_end - cs)
        bp = bv[rs] * rem
        kp = k[rs] * rem
        p_end = jnp.exp(cs_end)
        vc = v[rs]
        for q in range(PAIRS):
            ls = slice(q * LANES, (q + 1) * LANES)
            w = dict(rs=rs, ls=ls, q=q, p_end=p_end[:, ls], rts=_stack_heads(rt[:, ls], first_head))
            for name, z in (("ats", at), ("bts", bt), ("kts", kt), ("bps", bp), ("kps", kp), ("vs", vc)):
                w[name] = _stack_heads(z[:, ls].astype(BF16), first_head)
            work.append(w)

    for w in work:
        g = _dot(jnp.concatenate([w["ats"], w["rts"].astype(BF16)], axis=0),
                 jnp.concatenate([w["bts"], w["kts"]], axis=0), _NT) * gmask
        w["gb"] = g.astype(BF16)
        w["tinv"] = eye + g[:LANES, :LANES]
    for w in work:
        a_ab = w["gb"][:LANES, :LANES]
        w["pw"] = _dot(a_ab, a_ab).astype(BF16)
    for level in range(1, 6):
        for w in work:
            if level < 5:
                both = _dot(jnp.concatenate([w["pw"], w["tinv"].astype(BF16)], axis=0), w["pw"])
                w["pw"] = both[:LANES].astype(BF16)
                w["tinv"] = w["tinv"] + both[LANES:]
            else:
                w["tinv"] = w["tinv"] + _dot(w["tinv"].astype(BF16), w["pw"])
    for w in work:
        w["avs"] = _dot(w["gb"][:LANES, LANES:], w["vs"]).astype(BF16)
    for w in work:
        wu = _dot(w["tinv"].astype(BF16), jnp.concatenate([w["ats"], w["avs"]], axis=1)).astype(BF16)
        w["w_s"] = wu[:, :LANES]
        w["uv"] = jnp.concatenate([wu[:, LANES:], w["vs"]], axis=0)
    for w in work:
        qp_s = w["rts"] + _dot(w["gb"][LANES:, :LANES], w["w_s"])
        w["qp"] = (qp_s[:CHUNK] + qp_s[CHUNK:]).astype(BF16)
    for w in work:
        y0_s = _dot(w["gb"][LANES:], w["uv"])
        w["y0"] = y0_s[:CHUNK] + y0_s[CHUNK:]
    for w in work:
        w["m"] = _dot(w["bps"], w["w_s"], _TN).astype(BF16)
    for w in work:
        w["n_t"] = _dot(w["uv"], jnp.concatenate([w["bps"], w["kps"]], axis=0), _TN)
    for w in work:
        s = st_ref[w["q"]]
        sb = s.astype(BF16)
        yn_ref[w["rs"], w["ls"]] = _dot(w["qp"], sb, _NT) + w["y0"]
        st_ref[w["q"]] = s * w["p_end"] + _dot(sb, w["m"], _NT) + w["n_t"]

    y = yn_ref[...]
    yc = y - _segsum(y, seg2_ref) * (1.0 / HEAD_DIM)
    var = _segsum(yc * yc, seg2_ref) * (1.0 / HEAD_DIM)
    y = yc * lax.rsqrt(var + GN_EPS) * lng_ref[...] + lnb_ref[...] + bonus
    y_ref[0] = (_silu(gate_ref[0]) * y).astype(BF16)


def _rwkv(pbs, gate, v_first, prm, vres):
    bsz, t, _ = pbs.shape
    nch = RWKV_CHUNKS
    tb = nch * CHUNK
    has_vres = vres is not None

    tt = jnp.arange(CHUNK)
    tri = (tt[:, None] >= tt[None, :]).astype(BF16)
    tri3 = jnp.concatenate([tri, tri, tri], axis=1)
    idx = jnp.arange(2 * LANES)
    part, blk, pos = idx // LANES, (idx % LANES) // CHUNK, idx % CHUNK
    allowed = (blk[:, None] == blk[None, :]) & (
        (pos[None, :] < pos[:, None]) | ((part[:, None] == 1) & (pos[None, :] == pos[:, None])))
    gmask = allowed.astype(F32)
    eye = jnp.eye(LANES, dtype=F32)

    consts = [_vec(prm["mu_shift"]), _vec(prm["w_decay0"]), prm["w_decay2"].astype(BF16), _vec(prm["a0"]),
              prm["a2"].astype(BF16), _vec(prm["k_k"]), _vec(prm["k_a"]), _vec(prm["r_k"]), _vec(prm["lnx_g"]),
              _vec(prm["lnx_b"]), _pair_seg2(), tri3, gmask, eye]
    args = [pbs, gate]
    specs = [_row(tb, B_SHIFT_COLS), _row(tb, B_WIDTH)]
    out_specs = [_row(tb, B_WIDTH)]
    out_shape = [jax.ShapeDtypeStruct((bsz, t, B_WIDTH), BF16)]
    if has_vres:
        v0, v1, v2 = vres
        extra = [_vec(v0), v1.astype(BF16), v2.astype(BF16)]
        args += [v_first] + extra
        specs += [_row(tb, B_WIDTH)] + [_full(a) for a in extra]
    else:
        out_specs.append(_row(tb, B_WIDTH))
        out_shape.append(jax.ShapeDtypeStruct((bsz, t, B_WIDTH), F32))
    args += consts
    specs += [_full(a) for a in consts]
    return pl.pallas_call(
        functools.partial(_rwkv_kernel, has_vres, nch),
        grid=(bsz, t // tb),
        in_specs=specs,
        out_specs=out_specs,
        out_shape=out_shape,
        scratch_shapes=[pltpu.VMEM((tb, B_WIDTH), F32), pltpu.VMEM((8, B_SHIFT_COLS), F32),
                        pltpu.VMEM((PAIRS, LANES, LANES), F32)],
        name="rwkv7_vres" if has_vres else "rwkv7",
        compiler_params=_GRID_PARAMS,
    )(*args)


def _attn_kernel(q_ref, kvp_ref, kvc_ref, gate_ref, e_ref, o_ref, kv_ref, bias_ref):
    tq = ATTN_TQ

    @pl.when((pl.program_id(0) == 0) & (pl.program_id(1) == 0))
    def _():
        key = lax.broadcasted_iota(jnp.int32, (CHUNK, BAND), 1)
        for h in range(C_HEADS):
            rows = jnp.broadcast_to(e_ref[h:h + 1, :], (CHUNK, BIAS_EXT))
            tile = pltpu.roll(rows, 0, 1, stride=1, stride_axis=0)[:, :BAND]
            hs = slice((h % 2) * CHUNK, (h % 2 + 1) * CHUNK)
            bias_ref[0, h // 2, hs, :] = tile
            for c in range(tq // CHUNK):
                bias_ref[1 + c, h // 2, hs, :] = jnp.where(key >= tq - c * CHUNK, tile, -jnp.inf)

    kv_ref[0:tq] = kvp_ref[0]
    kv_ref[tq:2 * tq] = kvc_ref[0]
    first_block = pl.program_id(1) == 0
    first_head = lax.broadcasted_iota(jnp.int32, (CHUNK, LANES), 1) < HEAD_DIM
    scale = HEAD_DIM ** -0.5

    def scores(c):
        rs = slice(c * CHUNK, (c + 1) * CHUNK)
        band = slice(c * CHUNK, c * CHUNK + BAND)
        table = jnp.where(first_block, 1 + c, 0)
        out = []
        for pr in range(PAIRS):
            ls = slice(pr * LANES, (pr + 1) * LANES)
            qs = _stack_heads(q_ref[0, rs, ls] * scale, first_head)
            out.append(_dot(qs, kv_ref[band, ls], _NT) + bias_ref[table, pr])
        return out

    def finish(c, ss):
        rs = slice(c * CHUNK, (c + 1) * CHUNK)
        band = slice(c * CHUNK, c * CHUNK + BAND)
        for pr, s in enumerate(ss):
            ls = slice(pr * LANES, (pr + 1) * LANES)
            vb = kv_ref[band, C_WIDTH + pr * LANES:C_WIDTH + (pr + 1) * LANES]
            e = jnp.exp(s - jnp.max(s, axis=-1, keepdims=True))
            o = _dot(e.astype(BF16), vb) / jnp.sum(e, axis=-1, keepdims=True)
            y = jnp.where(first_head, o[:CHUNK], o[CHUNK:])
            o_ref[0, rs, ls] = (_silu(gate_ref[0, rs, ls]) * y).astype(BF16)

    pending = scores(0)
    for c in range(1, tq // CHUNK):
        nxt = scores(c)
        finish(c - 1, pending)
        pending = nxt
    finish(tq // CHUNK - 1, pending)


def _attention(q, kv, gate, rel_table):
    bsz, t, _ = q.shape
    tq = ATTN_TQ
    pad = LEFT_CHUNKS * CHUNK
    m = np.arange(BIAS_EXT)
    dist = np.where(m < BAND, pad - m, pad + BIAS_EXT - m)
    e = rel_table[:, np.clip(dist, -REL_CLIP, REL_CLIP) + REL_CLIP].astype(F32)
    return pl.pallas_call(
        _attn_kernel,
        grid=(bsz, t // tq),
        in_specs=[
            _row(tq, C_WIDTH),
            pl.BlockSpec((1, tq, 2 * C_WIDTH), lambda b, i: (b, jnp.maximum(i - 1, 0), 0)),
            _row(tq, 2 * C_WIDTH),
            _row(tq, C_WIDTH),
            _full(e),
        ],
        out_specs=_row(tq, C_WIDTH),
        out_shape=jax.ShapeDtypeStruct((bsz, t, C_WIDTH), BF16),
        scratch_shapes=[pltpu.VMEM((2 * tq, 2 * C_WIDTH), BF16),
                        pltpu.VMEM((1 + tq // CHUNK, PAIRS, 2 * CHUNK, BAND), F32)],
        name="band_attn",
        compiler_params=_GRID_PARAMS,
    )(q, kv, kv, gate, e)


def _out_kernel(alpha, ya_ref, yb_ref, yc_ref, x_ref, g1_ref, wout_ref, lng_ref, lnb_ref, o_ref):
    cat = jnp.concatenate([ya_ref[0], yb_ref[0], yc_ref[0]], axis=1)
    z = alpha * x_ref[0] + g1_ref[0] * _dot(cat, wout_ref[...])
    zc = z - jnp.mean(z, axis=-1, keepdims=True)
    var = jnp.mean(zc * zc, axis=-1, keepdims=True)
    o_ref[0] = zc * lax.rsqrt(var + LN_EPS) * lng_ref[...] + lnb_ref[...]


def _output(alpha, ya, yb, yc, x, gate1p, prm):
    bsz, t, _ = x.shape
    tm = OUT_TM
    consts = [prm["w_out"].astype(BF16), _vec(prm["ln_g"]), _vec(prm["ln_b"])]
    return pl.pallas_call(
        functools.partial(_out_kernel, alpha),
        grid=(bsz, t // tm),
        in_specs=[_row(tm, A_WIDTH), _row(tm, B_WIDTH), _row(tm, C_WIDTH), _row(tm, D_MODEL),
                  pl.BlockSpec((1, 1, D_MODEL), lambda b, i: (b, 0, 0))] + [_full(a) for a in consts],
        out_specs=_row(tm, D_MODEL),
        out_shape=jax.ShapeDtypeStruct((bsz, t, D_MODEL), F32),
        name="out_proj_norm",
        compiler_params=_GRID_PARAMS,
    )(ya, yb, yc, x, gate1p, *consts)


def kernel(x, c, w_ada, b_ada, w_in, sgu_ln_g, sgu_ln_b, w_spatial, b_spatial, mu_shift, w_decay0, w_decay2, a0, a2, k_k, k_a, r_k, lnx_g, lnx_b, v0, v1, v2, rel_bias, w_out, ln_g, ln_b):
    depth = w_ada.shape[0]
    alpha = (2 * depth) ** 0.25
    bsz = x.shape[0]
    stacked = dict(sgu_ln_g=sgu_ln_g, sgu_ln_b=sgu_ln_b, w_spatial=w_spatial, b_spatial=b_spatial,
                   mu_shift=mu_shift, w_decay0=w_decay0, w_decay2=w_decay2, a0=a0, a2=a2, k_k=k_k, k_a=k_a,
                   r_k=r_k, lnx_g=lnx_g, lnx_b=lnx_b, w_out=w_out, ln_g=ln_g, ln_b=ln_b)
    v_first = None
    for i in range(depth):
        prm = {name: val[i] for name, val in stacked.items()}
        mod = _modulation(c, w_ada[i], b_ada[i]).reshape(bsz, 3, 1, D_MODEL)
        shift, scale1p, gate1p = mod[:, 0], 1.0 + mod[:, 1], 1.0 + mod[:, 2]
        ya, pbs, gate_b, gate_c, q, kv = _projection(x, scale1p, shift, w_in[i], prm)
        if i == 0:
            yb, v_first = _rwkv(pbs, gate_b, None, prm, None)
        else:
            (yb,) = _rwkv(pbs, gate_b, v_first, prm, (v0[i - 1], v1[i - 1], v2[i - 1]))
        yc = _attention(q, kv, gate_c, rel_bias[i])
        x = _output(alpha, ya, yb, yc, x, gate1p, prm)
    return x
```

```python
import functools

import numpy as np

import jax
import jax.numpy as jnp
from jax import lax
from jax.experimental import pallas as pl
from jax.experimental.pallas import tpu as pltpu

F32 = jnp.float32
BF16 = jnp.bfloat16

D_MODEL = 1024
HEAD_DIM = 64
CHUNK = 64
SGU_BLOCK = 128
A_WIDTH = 256
B_WIDTH = 384
C_WIDTH = 384
C_HEADS = C_WIDTH // HEAD_DIM
LORA = 64
LEFT_CHUNKS = 8
BAND = (LEFT_CHUNKS + 1) * CHUNK
REL_CLIP = 256
A_COLS = 3 * A_WIDTH
B_SHIFT_COLS = 3 * B_WIDTH + 2 * LORA
B_COLS = B_SHIFT_COLS + B_WIDTH
C_COLS = 4 * C_WIDTH
PROJ_COLS = A_COLS + B_COLS + C_COLS
LN_EPS = 1e-5
GN_EPS = 64e-5

LANES = 128
PAIRS = B_WIDTH // LANES
VMEM_LIMIT = 56 * 1024 * 1024

PROJ_TM = 512
RWKV_CHUNKS = 8
ATTN_LEFT = LEFT_CHUNKS * CHUNK
ATTN_TQ = 2 * ATTN_LEFT
BIAS_EXT = -(-(BAND + CHUNK) // LANES) * LANES

_NN = (((1,), (0,)), ((), ()))
_NT = (((1,), (1,)), ((), ()))
_TN = (((0,), (0,)), ((), ()))


def _dot(a, b, dims=_NN):
    return lax.dot_general(a, b, dims, preferred_element_type=F32)


def _split2(x):
    hi = x.astype(BF16)
    lo = (x - hi.astype(F32)).astype(BF16)
    return hi, lo


def _split3(x):
    hi = x.astype(BF16)
    rem = x - hi.astype(F32)
    mid = rem.astype(BF16)
    lo = (rem - mid.astype(F32)).astype(BF16)
    return hi, mid, lo


def _segsum(x, seg2_ref):
    outs = []
    for q in range(x.shape[1] // LANES):
        hi, lo = _split2(x[:, q * LANES:(q + 1) * LANES])
        outs.append(_dot(jnp.concatenate([hi, lo], axis=1), seg2_ref[...]))
    return jnp.concatenate(outs, axis=1)


def _sigmoid(x):
    return 1.0 / (1.0 + jnp.exp(-x))


def _silu(x):
    return x * _sigmoid(x)


def _stack_heads(z, first_head):
    zero = jnp.zeros_like(z)
    return jnp.concatenate([jnp.where(first_head, z, zero), jnp.where(first_head, zero, z)], axis=0)


def _pair_seg2():
    head = jnp.arange(LANES) // HEAD_DIM
    seg = (head[:, None] == head[None, :]).astype(BF16)
    return jnp.concatenate([seg, seg], axis=0)


_row = lambda rows, w: pl.BlockSpec((1, rows, w), lambda b, i: (b, i, 0))
_full = lambda a: pl.BlockSpec(a.shape, lambda b, i: (0,) * a.ndim)
_vec = lambda a: a.reshape(1, -1)
_GRID_PARAMS = pltpu.CompilerParams(
    dimension_semantics=("arbitrary", "arbitrary"), vmem_limit_bytes=VMEM_LIMIT)


def _mod_kernel(c_ref, w_ref, b_ref, o_ref):
    c = c_ref[...]
    o_ref[...] = _dot(_silu(c).astype(BF16), w_ref[...]) + b_ref[...]


def _modulation(c, w_ada, b_ada):
    bsz = c.shape[0]
    return pl.pallas_call(
        _mod_kernel,
        out_shape=jax.ShapeDtypeStruct((bsz, 3 * D_MODEL), F32),
        name="adaln_mod",
        compiler_params=pltpu.CompilerParams(vmem_limit_bytes=VMEM_LIMIT),
    )(c, w_ada.astype(BF16), b_ada.reshape(1, -1))


def _proj_kernel(x_ref, sc_ref, sh_ref, w_ref, sg_ref, sb_ref, wsp_ref, bsp_ref, seg2_ref,
                 ya_ref, pbs_ref, gb_ref, gc_ref, q_ref, kv_ref):
    h = (x_ref[0] * sc_ref[0] + sh_ref[0]).astype(BF16)

    edges = np.cumsum([0, A_COLS, B_SHIFT_COLS, B_WIDTH + C_WIDTH, 3 * C_WIDTH])
    pa, pbs, gates, qkv = (_dot(h, w_ref[:, lo:hi]) for lo, hi in zip(edges[:-1], edges[1:]))
    pbs_ref[0] = pbs
    gb_ref[0] = gates[:, :B_WIDTH]
    gc_ref[0] = gates[:, B_WIDTH:]
    q_ref[0] = qkv[:, :C_WIDTH].astype(BF16)
    kv_ref[0] = qkv[:, C_WIDTH:].astype(BF16)

    u = pa[:, :A_WIDTH]
    v = pa[:, A_WIDTH:2 * A_WIDTH]
    ga = pa[:, 2 * A_WIDTH:]
    tm = v.shape[0]
    vc = v - _segsum(v, seg2_ref) * (1.0 / HEAD_DIM)
    var = _segsum(vc * vc, seg2_ref) * (1.0 / HEAD_DIM)
    vn = vc * lax.rsqrt(var + LN_EPS) * sg_ref[...] + sb_ref[...]

    first_head = lax.broadcasted_iota(jnp.int32, (SGU_BLOCK, LANES), 1) < HEAD_DIM
    prow = lax.broadcasted_iota(jnp.int32, (SGU_BLOCK, 2 * SGU_BLOCK), 0)
    pcol = lax.broadcasted_iota(jnp.int32, (SGU_BLOCK, 2 * SGU_BLOCK), 1) % SGU_BLOCK
    causal = prow >= pcol
    wsp = [jnp.where(causal, wsp_ref[gp], 0.0).astype(BF16) for gp in range(A_WIDTH // LANES)]
    blocks = []
    for n in range(tm // SGU_BLOCK):
        blk = vn[n * SGU_BLOCK:(n + 1) * SGU_BLOCK]
        cols = [_dot(wsp[gp], _stack_heads(blk[:, gp * LANES:(gp + 1) * LANES], first_head).astype(BF16))
                for gp in range(A_WIDTH // LANES)]
        blocks.append(jnp.concatenate(cols, axis=1) + bsp_ref[...])
    ya_ref[0] = (_silu(ga) * u * jnp.concatenate(blocks, axis=0)).astype(BF16)


def _projection(x, scale1p, shift, w_in, prm):
    bsz, t, _ = x.shape
    tm = PROJ_TM
    vec = pl.BlockSpec((1, 1, D_MODEL), lambda b, i: (b, 0, 0))
    w_sp = prm["w_spatial"]
    wsp = jnp.stack([jnp.concatenate([w_sp[2 * gp], w_sp[2 * gp + 1]], axis=1)
                     for gp in range(A_WIDTH // LANES)])
    bsp = jnp.repeat(prm["b_spatial"].T, HEAD_DIM, axis=1)
    c0 = A_COLS + B_COLS
    w_cols = jnp.concatenate(
        [w_in[:, :c0], w_in[:, c0 + 3 * C_WIDTH:], w_in[:, c0:c0 + 3 * C_WIDTH]], axis=1).astype(BF16)
    consts = [w_cols, _vec(prm["sgu_ln_g"]), _vec(prm["sgu_ln_b"]), wsp, bsp, _pair_seg2()]
    widths = [(A_WIDTH, BF16), (B_SHIFT_COLS, F32), (B_WIDTH, F32), (C_WIDTH, F32),
              (C_WIDTH, BF16), (2 * C_WIDTH, BF16)]
    return pl.pallas_call(
        _proj_kernel,
        grid=(bsz, t // tm),
        in_specs=[_row(tm, D_MODEL), vec, vec] + [_full(a) for a in consts],
        out_specs=[_row(tm, w) for w, _ in widths],
        out_shape=[jax.ShapeDtypeStruct((bsz, t, w), dt) for w, dt in widths],
        name="in_proj_sgu",
        compiler_params=_GRID_PARAMS,
    )(x, scale1p, shift, *consts)


def _rwkv_kernel(has_vres, nch, *refs):
    if has_vres:
        p_ref, gate_ref, vf_ref, v0_ref, v1_ref, v2_ref = refs[:6]
        refs = refs[6:]
    else:
        p_ref, gate_ref = refs[:2]
        refs = refs[2:]
    (mu_ref, w0_ref, w2_ref, a0_ref, a2_ref, kk_ref, ka_ref, rk_ref, lng_ref, lnb_ref,
     seg2_ref, tri3_ref, gmask_ref, eye_ref) = refs[:14]
    refs = refs[14:]
    if has_vres:
        y_ref, yn_ref, prev_ref, st_ref = refs
    else:
        y_ref, vraw_ref, yn_ref, prev_ref, st_ref = refs

    @pl.when(pl.program_id(1) == 0)
    def _():
        prev_ref[...] = jnp.zeros_like(prev_ref)
        st_ref[...] = jnp.zeros_like(st_ref)

    p = p_ref[0]
    tb = p.shape[0]
    row = lax.broadcasted_iota(jnp.int32, p.shape, 0)
    p_prev = jnp.where(row == 0, prev_ref[7:8, :], pltpu.roll(p, 1, 0))
    prev_ref[...] = p[tb - 8:tb, :]

    xs = p + (p_prev - p) * mu_ref[...]
    r = xs[:, 0:B_WIDTH]
    k = xs[:, B_WIDTH:2 * B_WIDTH]
    v = xs[:, 2 * B_WIDTH:3 * B_WIDTH]
    w_lo = xs[:, 3 * B_WIDTH:3 * B_WIDTH + LORA]
    a_lo = xs[:, 3 * B_WIDTH + LORA:]
    if not has_vres:
        vraw_ref[0] = v

    zw = w0_ref[...] + _dot(jnp.tanh(w_lo).astype(BF16), w2_ref[...])
    lw = _sigmoid(zw) * (-np.exp(-0.5).astype(np.float32))
    if has_vres:
        lora = _dot(_dot(v.astype(BF16), v1_ref[...]).astype(BF16), v2_ref[...])
        v = v + (vf_ref[0] - v) * _sigmoid(v0_ref[...] + lora)
    a = _sigmoid(a0_ref[...] + _dot(a_lo.astype(BF16), a2_ref[...]))
    kk = k * kk_ref[...]
    kk = kk * lax.rsqrt(jnp.maximum(_segsum(kk * kk, seg2_ref), 1e-24))
    k = k * (1.0 + (a - 1.0) * ka_ref[...])
    av = -kk
    bv = kk * a
    bonus = _segsum(r * k * rk_ref[...], seg2_ref) * v

    first_head = lax.broadcasted_iota(jnp.int32, (CHUNK, LANES), 1) < HEAD_DIM
    eye = eye_ref[...]
    gmask = gmask_ref[...]

    work = []
    for c in range(nch):
        rs = slice(c * CHUNK, (c + 1) * CHUNK)
        lw_c = lw[rs]
        cs = _dot(tri3_ref[...], jnp.concatenate(_split3(lw_c), axis=0))
        cs_end = cs[CHUNK - 1:CHUNK, :]
        at = av[rs] * jnp.exp(cs - lw_c)
        rt = r[rs] * jnp.exp(cs)
        inv = jnp.exp(-cs)
        bt = bv[rs] * inv
        kt = k[rs] * inv
        rem = jnp.exp(cs_end - cs)
        bp = bv[rs] * rem
        kp = k[rs] * rem
        p_end = jnp.exp(cs_end)
        vc = v[rs]
        for q in range(PAIRS):
            ls = slice(q * LANES, (q + 1) * LANES)
            w = dict(rs=rs, ls=ls, q=q, p_end=p_end[:, ls], rts=_stack_heads(rt[:, ls], first_head))
            for name, z in (("ats", at), ("bts", bt), ("kts", kt), ("bps", bp), ("kps", kp), ("vs", vc)):
                w[name] = _stack_heads(z[:, ls].astype(BF16), first_head)
            work.append(w)

    for w in work:
        g = _dot(jnp.concatenate([w["ats"], w["rts"].astype(BF16)], axis=0),
                 jnp.concatenate([w["bts"], w["kts"]], axis=0), _NT) * gmask
        w["gb"] = g.astype(BF16)
        w["tinv"] = eye + g[:LANES, :LANES]
    for w in work:
        a_ab = w["gb"][:LANES, :LANES]
        w["pw"] = _dot(a_ab, a_ab).astype(BF16)
    for level in range(1, 6):
        for w in work:
            if level < 5:
                both = _dot(jnp.concatenate([w["pw"], w["tinv"].astype(BF16)], axis=0), w["pw"])
                w["pw"] = both[:LANES].astype(BF16)
                w["tinv"] = w["tinv"] + both[LANES:]
            else:
                w["tinv"] = w["tinv"] + _dot(w["tinv"].astype(BF16), w["pw"])
    for w in work:
        w["avs"] = _dot(w["gb"][:LANES, LANES:], w["vs"]).astype(BF16)
    for w in work:
        wu = _dot(w["tinv"].astype(BF16), jnp.concatenate([w["ats"], w["avs"]], axis=1)).astype(BF16)
        w["w_s"] = wu[:, :LANES]
        w["uv"] = jnp.concatenate([wu[:, LANES:], w["vs"]], axis=0)
    for w in work:
        qp_s = w["rts"] + _dot(w["gb"][LANES:, :LANES], w["w_s"])
        w["qp"] = (qp_s[:CHUNK] + qp_s[CHUNK:]).astype(BF16)
    for w in work:
        y0_s = _dot(w["gb"][LANES:], w["uv"])
        w["y0"] = y0_s[:CHUNK] + y0_s[CHUNK:]
    for w in work:
        w["m"] = _dot(w["bps"], w["w_s"], _TN).astype(BF16)
    for w in work:
        w["n_t"] = _dot(w["uv"], jnp.concatenate([w["bps"], w["kps"]], axis=0), _TN)
    for w in work:
        s = st_ref[w["q"]]
        sb = s.astype(BF16)
        yn_ref[w["rs"], w["ls"]] = _dot(w["qp"], sb, _NT) + w["y0"]
        st_ref[w["q"]] = s * w["p_end"] + _dot(sb, w["m"], _NT) + w["n_t"]

    y = yn_ref[...]
    yc = y - _segsum(y, seg2_ref) * (1.0 / HEAD_DIM)
    var = _segsum(yc * yc, seg2_ref) * (1.0 / HEAD_DIM)
    y = yc * lax.rsqrt(var + GN_EPS) * lng_ref[...] + lnb_ref[...] + bonus
    y_ref[0] = (_silu(gate_ref[0]) * y).astype(BF16)


def _rwkv(pbs, gate, v_first, prm, vres):
    bsz, t, _ = pbs.shape
    nch = RWKV_CHUNKS
    tb = nch * CHUNK
    has_vres = vres is not None

    tt = jnp.arange(CHUNK)
    tri = (tt[:, None] >= tt[None, :]).astype(BF16)
    tri3 = jnp.concatenate([tri, tri, tri], axis=1)
    idx = jnp.arange(2 * LANES)
    part, blk, pos = idx // LANES, (idx % LANES) // CHUNK, idx % CHUNK
    allowed = (blk[:, None] == blk[None, :]) & (
        (pos[None, :] < pos[:, None]) | ((part[:, None] == 1) & (pos[None, :] == pos[:, None])))
    gmask = allowed.astype(F32)
    eye = jnp.eye(LANES, dtype=F32)

    consts = [_vec(prm["mu_shift"]), _vec(prm["w_decay0"]), prm["w_decay2"].astype(BF16), _vec(prm["a0"]),
              prm["a2"].astype(BF16), _vec(prm["k_k"]), _vec(prm["k_a"]), _vec(prm["r_k"]), _vec(prm["lnx_g"]),
              _vec(prm["lnx_b"]), _pair_seg2(), tri3, gmask, eye]
    args = [pbs, gate]
    specs = [_row(tb, B_SHIFT_COLS), _row(tb, B_WIDTH)]
    out_specs = [_row(tb, B_WIDTH)]
    out_shape = [jax.ShapeDtypeStruct((bsz, t, B_WIDTH), BF16)]
    if has_vres:
        v0, v1, v2 = vres
        extra = [_vec(v0), v1.astype(BF16), v2.astype(BF16)]
        args += [v_first] + extra
        specs += [_row(tb, B_WIDTH)] + [_full(a) for a in extra]
    else:
        out_specs.append(_row(tb, B_WIDTH))
        out_shape.append(jax.ShapeDtypeStruct((bsz, t, B_WIDTH), F32))
    args += consts
    specs += [_full(a) for a in consts]
    return pl.pallas_call(
        functools.partial(_rwkv_kernel, has_vres, nch),
        grid=(bsz, t // tb),
        in_specs=specs,
        out_specs=out_specs,
        out_shape=out_shape,
        scratch_shapes=[pltpu.VMEM((tb, B_WIDTH), F32), pltpu.VMEM((8, B_SHIFT_COLS), F32),
                        pltpu.VMEM((PAIRS, LANES, LANES), F32)],
        name="rwkv7_vres" if has_vres else "rwkv7",
        compiler_params=_GRID_PARAMS,
    )(*args)


def _attn_out_kernel(alpha, q_ref, kvp_ref, kvc_ref, gate_ref, e_ref, ya_ref, yb_ref, x_ref, g1_ref,
                     wout_ref, lng_ref, lnb_ref, o_ref, kv_ref, bias_ref, yc_ref):
    tq = ATTN_TQ

    @pl.when((pl.program_id(0) == 0) & (pl.program_id(1) == 0))
    def _():
        key = lax.broadcasted_iota(jnp.int32, (CHUNK, BAND), 1)
        for h in range(C_HEADS):
            rows = jnp.broadcast_to(e_ref[h:h + 1, :], (CHUNK, BIAS_EXT))
            tile = pltpu.roll(rows, 0, 1, stride=1, stride_axis=0)[:, :BAND]
            hs = slice((h % 2) * CHUNK, (h % 2 + 1) * CHUNK)
            bias_ref[0, h // 2, hs, :] = tile
            for c in range(LEFT_CHUNKS):
                bias_ref[1 + c, h // 2, hs, :] = jnp.where(key >= ATTN_LEFT - c * CHUNK, tile, -jnp.inf)

    kv_ref[0:ATTN_LEFT] = kvp_ref[0]
    kv_ref[ATTN_LEFT:ATTN_LEFT + tq] = kvc_ref[0]
    first_block = pl.program_id(1) == 0
    first_head = lax.broadcasted_iota(jnp.int32, (CHUNK, LANES), 1) < HEAD_DIM
    scale = HEAD_DIM ** -0.5

    def scores(c):
        rs = slice(c * CHUNK, (c + 1) * CHUNK)
        band = slice(c * CHUNK, c * CHUNK + BAND)
        table = jnp.where(first_block, 1 + c, 0) if c < LEFT_CHUNKS else 0
        out = []
        for pr in range(PAIRS):
            ls = slice(pr * LANES, (pr + 1) * LANES)
            qs = _stack_heads(q_ref[0, rs, ls] * scale, first_head)
            out.append(_dot(qs, kv_ref[band, ls], _NT) + bias_ref[table, pr])
        return out

    def finish(c, ss):
        rs = slice(c * CHUNK, (c + 1) * CHUNK)
        band = slice(c * CHUNK, c * CHUNK + BAND)
        for pr, s in enumerate(ss):
            ls = slice(pr * LANES, (pr + 1) * LANES)
            vb = kv_ref[band, C_WIDTH + pr * LANES:C_WIDTH + (pr + 1) * LANES]
            e = jnp.exp(s - jnp.max(s, axis=-1, keepdims=True))
            o = _dot(e.astype(BF16), vb) / jnp.sum(e, axis=-1, keepdims=True)
            y = jnp.where(first_head, o[:CHUNK], o[CHUNK:])
            yc_ref[rs, ls] = (_silu(gate_ref[0, rs, ls]) * y).astype(BF16)

    pending = scores(0)
    for c in range(1, tq // CHUNK):
        nxt = scores(c)
        finish(c - 1, pending)
        pending = nxt
    finish(tq // CHUNK - 1, pending)

    cat = jnp.concatenate([ya_ref[0], yb_ref[0], yc_ref[...]], axis=1)
    z = alpha * x_ref[0] + g1_ref[0] * _dot(cat, wout_ref[...])
    zc = z - jnp.mean(z, axis=-1, keepdims=True)
    var = jnp.mean(zc * zc, axis=-1, keepdims=True)
    o_ref[0] = zc * lax.rsqrt(var + LN_EPS) * lng_ref[...] + lnb_ref[...]


def _attention_output(alpha, q, kv, gate, rel_table, ya, yb, x, gate1p, prm):
    bsz, t, _ = q.shape
    tq = ATTN_TQ
    pad = LEFT_CHUNKS * CHUNK
    m = np.arange(BIAS_EXT)
    dist = np.where(m < BAND, pad - m, pad + BIAS_EXT - m)
    e = rel_table[:, np.clip(dist, -REL_CLIP, REL_CLIP) + REL_CLIP].astype(F32)
    consts = [prm["w_out"].astype(BF16), _vec(prm["ln_g"]), _vec(prm["ln_b"])]
    return pl.pallas_call(
        functools.partial(_attn_out_kernel, alpha),
        grid=(bsz, t // tq),
        in_specs=[
            _row(tq, C_WIDTH),
            pl.BlockSpec((1, ATTN_LEFT, 2 * C_WIDTH),
                         lambda b, i: (b, jnp.maximum(i * (tq // ATTN_LEFT) - 1, 0), 0)),
            _row(tq, 2 * C_WIDTH),
            _row(tq, C_WIDTH),
            _full(e),
            _row(tq, A_WIDTH),
            _row(tq, B_WIDTH),
            _row(tq, D_MODEL),
            pl.BlockSpec((1, 1, D_MODEL), lambda b, i: (b, 0, 0)),
        ] + [_full(a) for a in consts],
        out_specs=_row(tq, D_MODEL),
        out_shape=jax.ShapeDtypeStruct((bsz, t, D_MODEL), F32),
        scratch_shapes=[pltpu.VMEM((ATTN_LEFT + tq, 2 * C_WIDTH), BF16),
                        pltpu.VMEM((1 + LEFT_CHUNKS, PAIRS, 2 * CHUNK, BAND), F32),
                        pltpu.VMEM((tq, C_WIDTH), BF16)],
        name="band_attn_out",
        compiler_params=_GRID_PARAMS,
    )(q, kv, kv, gate, e, ya, yb, x, gate1p, *consts)


def kernel(x, c, w_ada, b_ada, w_in, sgu_ln_g, sgu_ln_b, w_spatial, b_spatial, mu_shift, w_decay0, w_decay2, a0, a2, k_k, k_a, r_k, lnx_g, lnx_b, v0, v1, v2, rel_bias, w_out, ln_g, ln_b):
    depth = w_ada.shape[0]
    alpha = (2 * depth) ** 0.25
    bsz = x.shape[0]
    stacked = dict(sgu_ln_g=sgu_ln_g, sgu_ln_b=sgu_ln_b, w_spatial=w_spatial, b_spatial=b_spatial,
                   mu_shift=mu_shift, w_decay0=w_decay0, w_decay2=w_decay2, a0=a0, a2=a2, k_k=k_k, k_a=k_a,
                   r_k=r_k, lnx_g=lnx_g, lnx_b=lnx_b, w_out=w_out, ln_g=ln_g, ln_b=ln_b)
    v_first = None
    for i in range(depth):
        prm = {name: val[i] for name, val in stacked.items()}
        mod = _modulation(c, w_ada[i], b_ada[i]).reshape(bsz, 3, 1, D_MODEL)
        shift, scale1p, gate1p = mod[:, 0], 1.0 + mod[:, 1], 1.0 + mod[:, 2]
        ya, pbs, gate_b, gate_c, q, kv = _projection(x, scale1p, shift, w_in[i], prm)
        if i == 0:
            yb, v_first = _rwkv(pbs, gate_b, None, prm, None)
        else:
            (yb,) = _rwkv(pbs, gate_b, v_first, prm, (v0[i - 1], v1[i - 1], v2[i - 1]))
        x = _attention_output(alpha, q, kv, gate_c, rel_bias[i], ya, yb, x, gate1p, prm)
    return x
```

```python
import functools

import numpy as np

import jax
import jax.numpy as jnp
from jax import lax
from jax.experimental import pallas as pl
from jax.experimental.pallas import tpu as pltpu

F32 = jnp.float32
BF16 = jnp.bfloat16

D_MODEL = 1024
HEAD_DIM = 64
CHUNK = 64
SGU_BLOCK = 128
A_WIDTH = 256
B_WIDTH = 384
C_WIDTH = 384
C_HEADS = C_WIDTH // HEAD_DIM
LORA = 64
LEFT_CHUNKS = 8
BAND = (LEFT_CHUNKS + 1) * CHUNK
REL_CLIP = 256
A_COLS = 3 * A_WIDTH
B_SHIFT_COLS = 3 * B_WIDTH + 2 * LORA
B_COLS = B_SHIFT_COLS + B_WIDTH
C_COLS = 4 * C_WIDTH
PROJ_COLS = A_COLS + B_COLS + C_COLS
LN_EPS = 1e-5
GN_EPS = 64e-5

LANES = 128
PAIRS = B_WIDTH // LANES
VMEM_LIMIT = 56 * 1024 * 1024

PROJ_TM = 512
RWKV_CHUNKS = 8
ATTN_LEFT = LEFT_CHUNKS * CHUNK
ATTN_TQ = 2 * ATTN_LEFT
BIAS_EXT = -(-(BAND + CHUNK) // LANES) * LANES

_NN = (((1,), (0,)), ((), ()))
_NT = (((1,), (1,)), ((), ()))
_TN = (((0,), (0,)), ((), ()))


def _dot(a, b, dims=_NN):
    return lax.dot_general(a, b, dims, preferred_element_type=F32)


def _split2(x):
    hi = x.astype(BF16)
    lo = (x - hi.astype(F32)).astype(BF16)
    return hi, lo


def _split3(x):
    hi = x.astype(BF16)
    rem = x - hi.astype(F32)
    mid = rem.astype(BF16)
    lo = (rem - mid.astype(F32)).astype(BF16)
    return hi, mid, lo


def _segsum(x, seg2_ref):
    outs = []
    for q in range(x.shape[1] // LANES):
        hi, lo = _split2(x[:, q * LANES:(q + 1) * LANES])
        outs.append(_dot(jnp.concatenate([hi, lo], axis=1), seg2_ref[...]))
    return jnp.concatenate(outs, axis=1)


def _sigmoid(x):
    return 1.0 / (1.0 + jnp.exp(-x))


def _silu(x):
    return x * _sigmoid(x)


def _stack_heads(z, first_head):
    zero = jnp.zeros_like(z)
    return jnp.concatenate([jnp.where(first_head, z, zero), jnp.where(first_head, zero, z)], axis=0)


def _pair_seg2():
    head = jnp.arange(LANES) // HEAD_DIM
    seg = (head[:, None] == head[None, :]).astype(BF16)
    return jnp.concatenate([seg, seg], axis=0)


_row = lambda rows, w: pl.BlockSpec((1, rows, w), lambda b, i: (b, i, 0))
_full = lambda a: pl.BlockSpec(a.shape, lambda b, i: (0,) * a.ndim)
_vec = lambda a: a.reshape(1, -1)
_GRID_PARAMS = pltpu.CompilerParams(
    dimension_semantics=("arbitrary", "arbitrary"), vmem_limit_bytes=VMEM_LIMIT)


def _mod_kernel(c_ref, w_ref, b_ref, o_ref):
    c = c_ref[...]
    o_ref[...] = _dot(_silu(c).astype(BF16), w_ref[...]) + b_ref[...]


def _modulation(c, w_ada, b_ada):
    bsz = c.shape[0]
    return pl.pallas_call(
        _mod_kernel,
        out_shape=jax.ShapeDtypeStruct((bsz, 3 * D_MODEL), F32),
        name="adaln_mod",
        compiler_params=pltpu.CompilerParams(vmem_limit_bytes=VMEM_LIMIT),
    )(c, w_ada.astype(BF16), b_ada.reshape(1, -1))


def _proj_kernel(x_ref, sc_ref, sh_ref, w_ref, mu_ref, sg_ref, sb_ref, wsp_ref, bsp_ref, seg2_ref,
                 ya_ref, xs_ref, gb_ref, gc_ref, q_ref, kv_ref, prev_ref):
    @pl.when(pl.program_id(1) == 0)
    def _():
        prev_ref[...] = jnp.zeros_like(prev_ref)

    h = (x_ref[0] * sc_ref[0] + sh_ref[0]).astype(BF16)

    edges = np.cumsum([0, A_COLS, B_SHIFT_COLS, B_WIDTH + C_WIDTH, 3 * C_WIDTH])
    pa, p, gates, qkv = (_dot(h, w_ref[:, lo:hi]) for lo, hi in zip(edges[:-1], edges[1:]))

    row = lax.broadcasted_iota(jnp.int32, p.shape, 0)
    p_prev = jnp.where(row == 0, prev_ref[7:8, :], pltpu.roll(p, 1, 0))
    prev_ref[...] = p[p.shape[0] - 8:, :]
    xs_ref[0] = p + (p_prev - p) * mu_ref[...]

    gb_ref[0] = _silu(gates[:, :B_WIDTH])
    gc_ref[0] = _silu(gates[:, B_WIDTH:])
    q_ref[0] = qkv[:, :C_WIDTH].astype(BF16)
    kv_ref[0] = qkv[:, C_WIDTH:].astype(BF16)

    u = pa[:, :A_WIDTH]
    v = pa[:, A_WIDTH:2 * A_WIDTH]
    ga = pa[:, 2 * A_WIDTH:]
    tm = v.shape[0]
    vc = v - _segsum(v, seg2_ref) * (1.0 / HEAD_DIM)
    var = _segsum(vc * vc, seg2_ref) * (1.0 / HEAD_DIM)
    vn = vc * lax.rsqrt(var + LN_EPS) * sg_ref[...] + sb_ref[...]

    first_head = lax.broadcasted_iota(jnp.int32, (SGU_BLOCK, LANES), 1) < HEAD_DIM
    prow = lax.broadcasted_iota(jnp.int32, (SGU_BLOCK, 2 * SGU_BLOCK), 0)
    pcol = lax.broadcasted_iota(jnp.int32, (SGU_BLOCK, 2 * SGU_BLOCK), 1) % SGU_BLOCK
    causal = prow >= pcol
    wsp = [jnp.where(causal, wsp_ref[gp], 0.0).astype(BF16) for gp in range(A_WIDTH // LANES)]
    blocks = []
    for n in range(tm // SGU_BLOCK):
        blk = vn[n * SGU_BLOCK:(n + 1) * SGU_BLOCK]
        cols = [_dot(wsp[gp], _stack_heads(blk[:, gp * LANES:(gp + 1) * LANES], first_head).astype(BF16))
                for gp in range(A_WIDTH // LANES)]
        blocks.append(jnp.concatenate(cols, axis=1) + bsp_ref[...])
    ya_ref[0] = (_silu(ga) * u * jnp.concatenate(blocks, axis=0)).astype(BF16)


def _projection(x, scale1p, shift, w_in, prm):
    bsz, t, _ = x.shape
    tm = PROJ_TM
    vec = pl.BlockSpec((1, 1, D_MODEL), lambda b, i: (b, 0, 0))
    w_sp = prm["w_spatial"]
    wsp = jnp.stack([jnp.concatenate([w_sp[2 * gp], w_sp[2 * gp + 1]], axis=1)
                     for gp in range(A_WIDTH // LANES)])
    bsp = jnp.repeat(prm["b_spatial"].T, HEAD_DIM, axis=1)
    c0 = A_COLS + B_COLS
    w_cols = jnp.concatenate(
        [w_in[:, :c0], w_in[:, c0 + 3 * C_WIDTH:], w_in[:, c0:c0 + 3 * C_WIDTH]], axis=1).astype(BF16)
    consts = [w_cols, _vec(prm["mu_shift"]), _vec(prm["sgu_ln_g"]), _vec(prm["sgu_ln_b"]), wsp, bsp,
              _pair_seg2()]
    widths = [(A_WIDTH, BF16), (B_SHIFT_COLS, F32), (B_WIDTH, F32), (C_WIDTH, F32),
              (C_WIDTH, BF16), (2 * C_WIDTH, BF16)]
    return pl.pallas_call(
        _proj_kernel,
        grid=(bsz, t // tm),
        in_specs=[_row(tm, D_MODEL), vec, vec] + [_full(a) for a in consts],
        out_specs=[_row(tm, w) for w, _ in widths],
        out_shape=[jax.ShapeDtypeStruct((bsz, t, w), dt) for w, dt in widths],
        scratch_shapes=[pltpu.VMEM((8, B_SHIFT_COLS), F32)],
        name="in_proj_sgu",
        compiler_params=_GRID_PARAMS,
    )(x, scale1p, shift, *consts)


def _rwkv_kernel(has_vres, nch, *refs):
    if has_vres:
        xs_ref, gate_ref, vf_ref, v0_ref, v1_ref, v2_ref = refs[:6]
        refs = refs[6:]
    else:
        xs_ref, gate_ref = refs[:2]
        refs = refs[2:]
    (w0_ref, w2_ref, a0_ref, a2_ref, kk_ref, ka_ref, rk_ref, lng_ref, lnb_ref,
     seg2_ref, tri3_ref, gmask_ref, eye_ref) = refs[:13]
    refs = refs[13:]
    if has_vres:
        y_ref, yn_ref, st_ref = refs
    else:
        y_ref, vraw_ref, yn_ref, st_ref = refs

    @pl.when(pl.program_id(1) == 0)
    def _():
        st_ref[...] = jnp.zeros_like(st_ref)

    xs = xs_ref[0]
    r = xs[:, 0:B_WIDTH]
    k = xs[:, B_WIDTH:2 * B_WIDTH]
    v = xs[:, 2 * B_WIDTH:3 * B_WIDTH]
    w_lo = xs[:, 3 * B_WIDTH:3 * B_WIDTH + LORA]
    a_lo = xs[:, 3 * B_WIDTH + LORA:]
    if not has_vres:
        vraw_ref[0] = v

    zw = w0_ref[...] + _dot(jnp.tanh(w_lo).astype(BF16), w2_ref[...])
    lw = _sigmoid(zw) * (-np.exp(-0.5).astype(np.float32))
    if has_vres:
        lora = _dot(_dot(v.astype(BF16), v1_ref[...]).astype(BF16), v2_ref[...])
        v = v + (vf_ref[0] - v) * _sigmoid(v0_ref[...] + lora)
    a = _sigmoid(a0_ref[...] + _dot(a_lo.astype(BF16), a2_ref[...]))
    kk = k * kk_ref[...]
    kk = kk * lax.rsqrt(jnp.maximum(_segsum(kk * kk, seg2_ref), 1e-24))
    k = k * (1.0 + (a - 1.0) * ka_ref[...])
    av = -kk
    bv = kk * a
    bonus = _segsum(r * k * rk_ref[...], seg2_ref) * v

    first_head = lax.broadcasted_iota(jnp.int32, (CHUNK, LANES), 1) < HEAD_DIM
    eye = eye_ref[...]
    gmask = gmask_ref[...]

    work = []
    for c in range(nch):
        rs = slice(c * CHUNK, (c + 1) * CHUNK)
        lw_c = lw[rs]
        cs = _dot(tri3_ref[...], jnp.concatenate(_split3(lw_c), axis=0))
        cs_end = cs[CHUNK - 1:CHUNK, :]
        at = av[rs] * jnp.exp(cs - lw_c)
        rt = r[rs] * jnp.exp(cs)
        inv = jnp.exp(-cs)
        bt = bv[rs] * inv
        kt = k[rs] * inv
        rem = jnp.exp(cs_end - cs)
        bp = bv[rs] * rem
        kp = k[rs] * rem
        p_end = jnp.exp(cs_end)
        vc = v[rs]
        for q in range(PAIRS):
            ls = slice(q * LANES, (q + 1) * LANES)
            w = dict(rs=rs, ls=ls, q=q, p_end=p_end[:, ls], rts=_stack_heads(rt[:, ls], first_head))
            for name, z in (("ats", at), ("bts", bt), ("kts", kt), ("bps", bp), ("kps", kp), ("vs", vc)):
                w[name] = _stack_heads(z[:, ls].astype(BF16), first_head)
            work.append(w)

    for w in work:
        g = _dot(jnp.concatenate([w["ats"], w["rts"].astype(BF16)], axis=0),
                 jnp.concatenate([w["bts"], w["kts"]], axis=0), _NT) * gmask
        w["gb"] = g.astype(BF16)
        w["tinv"] = eye + g[:LANES, :LANES]
    for w in work:
        a_ab = w["gb"][:LANES, :LANES]
        w["pw"] = _dot(a_ab, a_ab).astype(BF16)
    for level in range(1, 6):
        for w in work:
            if level < 5:
                both = _dot(jnp.concatenate([w["pw"], w["tinv"].astype(BF16)], axis=0), w["pw"])
                w["pw"] = both[:LANES].astype(BF16)
                w["tinv"] = w["tinv"] + both[LANES:]
            else:
                w["tinv"] = w["tinv"] + _dot(w["tinv"].astype(BF16), w["pw"])
    for w in work:
        w["avs"] = _dot(w["gb"][:LANES, LANES:], w["vs"]).astype(BF16)
    for w in work:
        wu = _dot(w["tinv"].astype(BF16), jnp.concatenate([w["ats"], w["avs"]], axis=1)).astype(BF16)
        w["w_s"] = wu[:, :LANES]
        w["uv"] = jnp.concatenate([wu[:, LANES:], w["vs"]], axis=0)
    for w in work:
        qp_s = w["rts"] + _dot(w["gb"][LANES:, :LANES], w["w_s"])
        w["qp"] = (qp_s[:CHUNK] + qp_s[CHUNK:]).astype(BF16)
    for w in work:
        y0_s = _dot(w["gb"][LANES:], w["uv"])
        w["y0"] = y0_s[:CHUNK] + y0_s[CHUNK:]
    for w in work:
        w["m"] = _dot(w["bps"], w["w_s"], _TN).astype(BF16)
    for w in work:
        w["n_t"] = _dot(w["uv"], jnp.concatenate([w["bps"], w["kps"]], axis=0), _TN)
    for w in work:
        s = st_ref[w["q"]]
        sb = s.astype(BF16)
        yn_ref[w["rs"], w["ls"]] = _dot(w["qp"], sb, _NT) + w["y0"]
        st_ref[w["q"]] = s * w["p_end"] + _dot(sb, w["m"], _NT) + w["n_t"]

    y = yn_ref[...]
    yc = y - _segsum(y, seg2_ref) * (1.0 / HEAD_DIM)
    var = _segsum(yc * yc, seg2_ref) * (1.0 / HEAD_DIM)
    y = yc * lax.rsqrt(var + GN_EPS) * lng_ref[...] + lnb_ref[...] + bonus
    y_ref[0] = (gate_ref[0] * y).astype(BF16)


def _rwkv(pbs, gate, v_first, prm, vres):
    bsz, t, _ = pbs.shape
    nch = RWKV_CHUNKS
    tb = nch * CHUNK
    has_vres = vres is not None

    tt = jnp.arange(CHUNK)
    tri = (tt[:, None] >= tt[None, :]).astype(BF16)
    tri3 = jnp.concatenate([tri, tri, tri], axis=1)
    idx = jnp.arange(2 * LANES)
    part, blk, pos = idx // LANES, (idx % LANES) // CHUNK, idx % CHUNK
    allowed = (blk[:, None] == blk[None, :]) & (
        (pos[None, :] < pos[:, None]) | ((part[:, None] == 1) & (pos[None, :] == pos[:, None])))
    gmask = allowed.astype(F32)
    eye = jnp.eye(LANES, dtype=F32)

    consts = [_vec(prm["w_decay0"]), prm["w_decay2"].astype(BF16), _vec(prm["a0"]),
              prm["a2"].astype(BF16), _vec(prm["k_k"]), _vec(prm["k_a"]), _vec(prm["r_k"]), _vec(prm["lnx_g"]),
              _vec(prm["lnx_b"]), _pair_seg2(), tri3, gmask, eye]
    args = [pbs, gate]
    specs = [_row(tb, B_SHIFT_COLS), _row(tb, B_WIDTH)]
    out_specs = [_row(tb, B_WIDTH)]
    out_shape = [jax.ShapeDtypeStruct((bsz, t, B_WIDTH), BF16)]
    if has_vres:
        v0, v1, v2 = vres
        extra = [_vec(v0), v1.astype(BF16), v2.astype(BF16)]
        args += [v_first] + extra
        specs += [_row(tb, B_WIDTH)] + [_full(a) for a in extra]
    else:
        out_specs.append(_row(tb, B_WIDTH))
        out_shape.append(jax.ShapeDtypeStruct((bsz, t, B_WIDTH), F32))
    args += consts
    specs += [_full(a) for a in consts]
    return pl.pallas_call(
        functools.partial(_rwkv_kernel, has_vres, nch),
        grid=(bsz, t // tb),
        in_specs=specs,
        out_specs=out_specs,
        out_shape=out_shape,
        scratch_shapes=[pltpu.VMEM((tb, B_WIDTH), F32),
                        pltpu.VMEM((PAIRS, LANES, LANES), F32)],
        name="rwkv7_vres" if has_vres else "rwkv7",
        compiler_params=_GRID_PARAMS,
    )(*args)


def _attn_out_kernel(alpha, q_ref, kvp_ref, kvc_ref, gate_ref, e_ref, ya_ref, yb_ref, x_ref, g1_ref,
                     wout_ref, lng_ref, lnb_ref, o_ref, kv_ref, bias_ref, yc_ref):
    tq = ATTN_TQ

    @pl.when((pl.program_id(0) == 0) & (pl.program_id(1) == 0))
    def _():
        key = lax.broadcasted_iota(jnp.int32, (CHUNK, BAND), 1)
        for h in range(C_HEADS):
            rows = jnp.broadcast_to(e_ref[h:h + 1, :], (CHUNK, BIAS_EXT))
            tile = pltpu.roll(rows, 0, 1, stride=1, stride_axis=0)[:, :BAND]
            hs = slice((h % 2) * CHUNK, (h % 2 + 1) * CHUNK)
            bias_ref[0, h // 2, hs, :] = tile
            for c in range(LEFT_CHUNKS):
                bias_ref[1 + c, h // 2, hs, :] = jnp.where(key >= ATTN_LEFT - c * CHUNK, tile, -jnp.inf)

    kv_ref[0:ATTN_LEFT] = kvp_ref[0]
    kv_ref[ATTN_LEFT:ATTN_LEFT + tq] = kvc_ref[0]
    first_block = pl.program_id(1) == 0
    first_head = lax.broadcasted_iota(jnp.int32, (CHUNK, LANES), 1) < HEAD_DIM
    scale = HEAD_DIM ** -0.5

    def scores(c):
        rs = slice(c * CHUNK, (c + 1) * CHUNK)
        band = slice(c * CHUNK, c * CHUNK + BAND)
        table = jnp.where(first_block, 1 + c, 0) if c < LEFT_CHUNKS else 0
        out = []
        for pr in range(PAIRS):
            ls = slice(pr * LANES, (pr + 1) * LANES)
            qs = _stack_heads(q_ref[0, rs, ls] * scale, first_head)
            out.append(_dot(qs, kv_ref[band, ls], _NT) + bias_ref[table, pr])
        return out

    def finish(c, ss):
        rs = slice(c * CHUNK, (c + 1) * CHUNK)
        band = slice(c * CHUNK, c * CHUNK + BAND)
        for pr, s in enumerate(ss):
            ls = slice(pr * LANES, (pr + 1) * LANES)
            vb = kv_ref[band, C_WIDTH + pr * LANES:C_WIDTH + (pr + 1) * LANES]
            e = jnp.exp(s - jnp.max(s, axis=-1, keepdims=True))
            o = _dot(e.astype(BF16), vb) / jnp.sum(e, axis=-1, keepdims=True)
            y = jnp.where(first_head, o[:CHUNK], o[CHUNK:])
            yc_ref[rs, ls] = (gate_ref[0, rs, ls] * y).astype(BF16)

    pending = scores(0)
    for c in range(1, tq // CHUNK):
        nxt = scores(c)
        finish(c - 1, pending)
        pending = nxt
    finish(tq // CHUNK - 1, pending)

    cat = jnp.concatenate([ya_ref[0], yb_ref[0], yc_ref[...]], axis=1)
    z = alpha * x_ref[0] + g1_ref[0] * _dot(cat, wout_ref[...])
    zc = z - jnp.mean(z, axis=-1, keepdims=True)
    var = jnp.mean(zc * zc, axis=-1, keepdims=True)
    o_ref[0] = zc * lax.rsqrt(var + LN_EPS) * lng_ref[...] + lnb_ref[...]


def _attention_output(alpha, q, kv, gate, rel_table, ya, yb, x, gate1p, prm):
    bsz, t, _ = q.shape
    tq = ATTN_TQ
    pad = LEFT_CHUNKS * CHUNK
    m = np.arange(BIAS_EXT)
    dist = np.where(m < BAND, pad - m, pad + BIAS_EXT - m)
    e = rel_table[:, np.clip(dist, -REL_CLIP, REL_CLIP) + REL_CLIP].astype(F32)
    consts = [prm["w_out"].astype(BF16), _vec(prm["ln_g"]), _vec(prm["ln_b"])]
    return pl.pallas_call(
        functools.partial(_attn_out_kernel, alpha),
        grid=(bsz, t // tq),
        in_specs=[
            _row(tq, C_WIDTH),
            pl.BlockSpec((1, ATTN_LEFT, 2 * C_WIDTH),
                         lambda b, i: (b, jnp.maximum(i * (tq // ATTN_LEFT) - 1, 0), 0)),
            _row(tq, 2 * C_WIDTH),
            _row(tq, C_WIDTH),
            _full(e),
            _row(tq, A_WIDTH),
            _row(tq, B_WIDTH),
            _row(tq, D_MODEL),
            pl.BlockSpec((1, 1, D_MODEL), lambda b, i: (b, 0, 0)),
        ] + [_full(a) for a in consts],
        out_specs=_row(tq, D_MODEL),
        out_shape=jax.ShapeDtypeStruct((bsz, t, D_MODEL), F32),
        scratch_shapes=[pltpu.VMEM((ATTN_LEFT + tq, 2 * C_WIDTH), BF16),
                        pltpu.VMEM((1 + LEFT_CHUNKS, PAIRS, 2 * CHUNK, BAND), F32),
                        pltpu.VMEM((tq, C_WIDTH), BF16)],
        name="band_attn_out",
        compiler_params=_GRID_PARAMS,
    )(q, kv, kv, gate, e, ya, yb, x, gate1p, *consts)


def kernel(x, c, w_ada, b_ada, w_in, sgu_ln_g, sgu_ln_b, w_spatial, b_spatial, mu_shift, w_decay0, w_decay2, a0, a2, k_k, k_a, r_k, lnx_g, lnx_b, v0, v1, v2, rel_bias, w_out, ln_g, ln_b):
    depth = w_ada.shape[0]
    alpha = (2 * depth) ** 0.25
    bsz = x.shape[0]
    stacked = dict(sgu_ln_g=sgu_ln_g, sgu_ln_b=sgu_ln_b, w_spatial=w_spatial, b_spatial=b_spatial,
                   mu_shift=mu_shift, w_decay0=w_decay0, w_decay2=w_decay2, a0=a0, a2=a2, k_k=k_k, k_a=k_a,
                   r_k=r_k, lnx_g=lnx_g, lnx_b=lnx_b, w_out=w_out, ln_g=ln_g, ln_b=ln_b)
    v_first = None
    for i in range(depth):
        prm = {name: val[i] for name, val in stacked.items()}
        mod = _modulation(c, w_ada[i], b_ada[i]).reshape(bsz, 3, 1, D_MODEL)
        shift, scale1p, gate1p = mod[:, 0], 1.0 + mod[:, 1], 1.0 + mod[:, 2]
        ya, pbs, gate_b, gate_c, q, kv = _projection(x, scale1p, shift, w_in[i], prm)
        if i == 0:
            yb, v_first = _rwkv(pbs, gate_b, None, prm, None)
        else:
            (yb,) = _rwkv(pbs, gate_b, v_first, prm, (v0[i - 1], v1[i - 1], v2[i - 1]))
        x = _attention_output(alpha, q, kv, gate_c, rel_bias[i], ya, yb, x, gate1p, prm)
    return x
```

```python
import functools

import numpy as np

import jax
import jax.numpy as jnp
from jax import lax
from jax.experimental import pallas as pl
from jax.experimental.pallas import tpu as pltpu

F32 = jnp.float32
BF16 = jnp.bfloat16

D_MODEL = 1024
HEAD_DIM = 64
CHUNK = 64
SGU_BLOCK = 128
A_WIDTH = 256
B_WIDTH = 384
C_WIDTH = 384
C_HEADS = C_WIDTH // HEAD_DIM
LORA = 64
LEFT_CHUNKS = 8
BAND = (LEFT_CHUNKS + 1) * CHUNK
REL_CLIP = 256
A_COLS = 3 * A_WIDTH
B_SHIFT_COLS = 3 * B_WIDTH + 2 * LORA
B_COLS = B_SHIFT_COLS + B_WIDTH
C_COLS = 4 * C_WIDTH
PROJ_COLS = A_COLS + B_COLS + C_COLS
LN_EPS = 1e-5
GN_EPS = 64e-5

LANES = 128
PAIRS = B_WIDTH // LANES
VMEM_LIMIT = 56 * 1024 * 1024

PROJ_TM = 512
RWKV_CHUNKS = 8
ATTN_LEFT = LEFT_CHUNKS * CHUNK
ATTN_TQ = 2 * ATTN_LEFT
BIAS_EXT = -(-(BAND + CHUNK) // LANES) * LANES

_NN = (((1,), (0,)), ((), ()))
_NT = (((1,), (1,)), ((), ()))
_TN = (((0,), (0,)), ((), ()))


def _dot(a, b, dims=_NN):
    return lax.dot_general(a, b, dims, preferred_element_type=F32)


def _split2(x):
    hi = x.astype(BF16)
    lo = (x - hi.astype(F32)).astype(BF16)
    return hi, lo


def _split3(x):
    hi = x.astype(BF16)
    rem = x - hi.astype(F32)
    mid = rem.astype(BF16)
    lo = (rem - mid.astype(F32)).astype(BF16)
    return hi, mid, lo


def _segsum(x, seg2_ref):
    outs = []
    for q in range(x.shape[1] // LANES):
        hi, lo = _split2(x[:, q * LANES:(q + 1) * LANES])
        outs.append(_dot(jnp.concatenate([hi, lo], axis=1), seg2_ref[...]))
    return jnp.concatenate(outs, axis=1)


def _sigmoid(x):
    return 1.0 / (1.0 + jnp.exp(-x))


def _silu(x):
    return x * _sigmoid(x)


def _stack_heads(z, first_head):
    zero = jnp.zeros_like(z)
    return jnp.concatenate([jnp.where(first_head, z, zero), jnp.where(first_head, zero, z)], axis=0)


def _pair_seg2():
    head = jnp.arange(LANES) // HEAD_DIM
    seg = (head[:, None] == head[None, :]).astype(BF16)
    return jnp.concatenate([seg, seg], axis=0)


_row = lambda rows, w: pl.BlockSpec((1, rows, w), lambda b, i: (b, i, 0))
_full = lambda a: pl.BlockSpec(a.shape, lambda b, i: (0,) * a.ndim)
_vec = lambda a: a.reshape(1, -1)
_GRID_PARAMS = pltpu.CompilerParams(
    dimension_semantics=("arbitrary", "arbitrary"), vmem_limit_bytes=VMEM_LIMIT)


def _mod_kernel(c_ref, w_ref, b_ref, o_ref):
    c = c_ref[...]
    o_ref[...] = _dot(_silu(c).astype(BF16), w_ref[...]) + b_ref[...]


def _modulation(c, w_ada, b_ada):
    bsz = c.shape[0]
    return pl.pallas_call(
        _mod_kernel,
        out_shape=jax.ShapeDtypeStruct((bsz, 3 * D_MODEL), F32),
        name="adaln_mod",
        compiler_params=pltpu.CompilerParams(vmem_limit_bytes=VMEM_LIMIT),
    )(c, w_ada.astype(BF16), b_ada.reshape(1, -1))


def _proj_kernel(x_ref, sc_ref, sh_ref, w_ref, mu_ref, sg_ref, sb_ref, wsp_ref, bsp_ref, seg2_ref,
                 ya_ref, xs_ref, gb_ref, gc_ref, q_ref, kv_ref, prev_ref):
    @pl.when(pl.program_id(1) == 0)
    def _():
        prev_ref[...] = jnp.zeros_like(prev_ref)

    h = (x_ref[0] * sc_ref[0] + sh_ref[0]).astype(BF16)

    edges = np.cumsum([0, A_COLS, B_SHIFT_COLS, B_WIDTH + C_WIDTH, 3 * C_WIDTH])
    pa, p, gates, qkv = (_dot(h, w_ref[:, lo:hi]) for lo, hi in zip(edges[:-1], edges[1:]))

    row = lax.broadcasted_iota(jnp.int32, p.shape, 0)
    p_prev = jnp.where(row == 0, prev_ref[7:8, :], pltpu.roll(p, 1, 0))
    prev_ref[...] = p[p.shape[0] - 8:, :]
    xs_ref[0] = p + (p_prev - p) * mu_ref[...]

    gb_ref[0] = _silu(gates[:, :B_WIDTH])
    gc_ref[0] = _silu(gates[:, B_WIDTH:])
    q_ref[0] = qkv[:, :C_WIDTH].astype(BF16)
    kv_ref[0] = qkv[:, C_WIDTH:].astype(BF16)

    u = pa[:, :A_WIDTH]
    v = pa[:, A_WIDTH:2 * A_WIDTH]
    ga = pa[:, 2 * A_WIDTH:]
    tm = v.shape[0]
    vc = v - _segsum(v, seg2_ref) * (1.0 / HEAD_DIM)
    var = _segsum(vc * vc, seg2_ref) * (1.0 / HEAD_DIM)
    vn = vc * lax.rsqrt(var + LN_EPS) * sg_ref[...] + sb_ref[...]

    first_head = lax.broadcasted_iota(jnp.int32, (SGU_BLOCK, LANES), 1) < HEAD_DIM
    prow = lax.broadcasted_iota(jnp.int32, (SGU_BLOCK, 2 * SGU_BLOCK), 0)
    pcol = lax.broadcasted_iota(jnp.int32, (SGU_BLOCK, 2 * SGU_BLOCK), 1) % SGU_BLOCK
    causal = prow >= pcol
    wsp = [jnp.where(causal, wsp_ref[gp], 0.0).astype(BF16) for gp in range(A_WIDTH // LANES)]
    blocks = []
    for n in range(tm // SGU_BLOCK):
        blk = vn[n * SGU_BLOCK:(n + 1) * SGU_BLOCK]
        cols = [_dot(wsp[gp], _stack_heads(blk[:, gp * LANES:(gp + 1) * LANES], first_head).astype(BF16))
                for gp in range(A_WIDTH // LANES)]
        blocks.append(jnp.concatenate(cols, axis=1) + bsp_ref[...])
    ya_ref[0] = (_silu(ga) * u * jnp.concatenate(blocks, axis=0)).astype(BF16)


def _projection(x, scale1p, shift, w_in, prm):
    bsz, t, _ = x.shape
    tm = PROJ_TM
    vec = pl.BlockSpec((1, 1, D_MODEL), lambda b, i: (b, 0, 0))
    w_sp = prm["w_spatial"]
    wsp = jnp.stack([jnp.concatenate([w_sp[2 * gp], w_sp[2 * gp + 1]], axis=1)
                     for gp in range(A_WIDTH // LANES)])
    bsp = jnp.repeat(prm["b_spatial"].T, HEAD_DIM, axis=1)
    c0 = A_COLS + B_COLS
    w_cols = jnp.concatenate(
        [w_in[:, :c0], w_in[:, c0 + 3 * C_WIDTH:], w_in[:, c0:c0 + 3 * C_WIDTH]], axis=1).astype(BF16)
    consts = [w_cols, _vec(prm["mu_shift"]), _vec(prm["sgu_ln_g"]), _vec(prm["sgu_ln_b"]), wsp, bsp,
              _pair_seg2()]
    widths = [(A_WIDTH, BF16), (B_SHIFT_COLS, F32), (B_WIDTH, F32), (C_WIDTH, F32),
              (C_WIDTH, BF16), (2 * C_WIDTH, BF16)]
    return pl.pallas_call(
        _proj_kernel,
        grid=(bsz, t // tm),
        in_specs=[_row(tm, D_MODEL), vec, vec] + [_full(a) for a in consts],
        out_specs=[_row(tm, w) for w, _ in widths],
        out_shape=[jax.ShapeDtypeStruct((bsz, t, w), dt) for w, dt in widths],
        scratch_shapes=[pltpu.VMEM((8, B_SHIFT_COLS), F32)],
        name="in_proj_sgu",
        compiler_params=_GRID_PARAMS,
    )(x, scale1p, shift, *consts)


def _rwkv_kernel(has_vres, nch, *refs):
    if has_vres:
        xs_ref, gate_ref, vf_ref, v1_ref, v2_ref = refs[:5]
        refs = refs[5:]
    else:
        xs_ref, gate_ref = refs[:2]
        refs = refs[2:]
    vec_ref, w2_ref, a2_ref, seg2_ref, tri3_ref, gmask_ref, eye_ref = refs[:7]
    refs = refs[7:]
    if has_vres:
        y_ref, yn_ref, st_ref = refs
    else:
        y_ref, vraw_ref, yn_ref, st_ref = refs

    @pl.when(pl.program_id(1) == 0)
    def _():
        st_ref[...] = jnp.zeros_like(st_ref)

    xs = xs_ref[0]
    vecs = vec_ref[...]
    w0, a0, k_k, k_a, r_k, ln_g, ln_b, v0 = (vecs[i:i + 1] for i in range(8))
    r = xs[:, 0:B_WIDTH]
    k = xs[:, B_WIDTH:2 * B_WIDTH]
    v = xs[:, 2 * B_WIDTH:3 * B_WIDTH]
    w_lo = xs[:, 3 * B_WIDTH:3 * B_WIDTH + LORA]
    a_lo = xs[:, 3 * B_WIDTH + LORA:]
    if not has_vres:
        vraw_ref[0] = v

    zw = w0 + _dot(jnp.tanh(w_lo).astype(BF16), w2_ref[...])
    lw = _sigmoid(zw) * (-np.exp(-0.5).astype(np.float32))
    if has_vres:
        lora = _dot(_dot(v.astype(BF16), v1_ref[...]).astype(BF16), v2_ref[...])
        v = v + (vf_ref[0] - v) * _sigmoid(v0 + lora)
    a = _sigmoid(a0 + _dot(a_lo.astype(BF16), a2_ref[...]))
    kk = k * k_k
    kk = kk * lax.rsqrt(jnp.maximum(_segsum(kk * kk, seg2_ref), 1e-24))
    k = k * (1.0 + (a - 1.0) * k_a)
    av = -kk
    bv = kk * a
    bonus = _segsum(r * k * r_k, seg2_ref) * v

    first_head = lax.broadcasted_iota(jnp.int32, (CHUNK, LANES), 1) < HEAD_DIM
    eye = eye_ref[...]
    gmask = gmask_ref[...]

    work = []
    for c in range(nch):
        rs = slice(c * CHUNK, (c + 1) * CHUNK)
        lw_c = lw[rs]
        cs = _dot(tri3_ref[...], jnp.concatenate(_split3(lw_c), axis=0))
        cs_end = cs[CHUNK - 1:CHUNK, :]
        at = av[rs] * jnp.exp(cs - lw_c)
        rt = r[rs] * jnp.exp(cs)
        inv = jnp.exp(-cs)
        bt = bv[rs] * inv
        kt = k[rs] * inv
        rem = jnp.exp(cs_end - cs)
        bp = bv[rs] * rem
        kp = k[rs] * rem
        p_end = jnp.exp(cs_end)
        vc = v[rs]
        for q in range(PAIRS):
            ls = slice(q * LANES, (q + 1) * LANES)
            w = dict(rs=rs, ls=ls, q=q, p_end=p_end[:, ls], rts=_stack_heads(rt[:, ls], first_head))
            for name, z in (("ats", at), ("bts", bt), ("kts", kt), ("bps", bp), ("kps", kp), ("vs", vc)):
                w[name] = _stack_heads(z[:, ls].astype(BF16), first_head)
            work.append(w)

    for w in work:
        g = _dot(jnp.concatenate([w["ats"], w["rts"].astype(BF16)], axis=0),
                 jnp.concatenate([w["bts"], w["kts"]], axis=0), _NT) * gmask
        w["gb"] = g.astype(BF16)
        w["tinv"] = eye + g[:LANES, :LANES]
    for w in work:
        a_ab = w["gb"][:LANES, :LANES]
        w["pw"] = _dot(a_ab, a_ab).astype(BF16)
    for level in range(1, 6):
        for w in work:
            if level < 5:
                both = _dot(jnp.concatenate([w["pw"], w["tinv"].astype(BF16)], axis=0), w["pw"])
                w["pw"] = both[:LANES].astype(BF16)
                w["tinv"] = w["tinv"] + both[LANES:]
            else:
                w["tinv"] = w["tinv"] + _dot(w["tinv"].astype(BF16), w["pw"])
    for w in work:
        w["avs"] = _dot(w["gb"][:LANES, LANES:], w["vs"]).astype(BF16)
    for w in work:
        wu = _dot(w["tinv"].astype(BF16), jnp.concatenate([w["ats"], w["avs"]], axis=1)).astype(BF16)
        w["w_s"] = wu[:, :LANES]
        w["uv"] = jnp.concatenate([wu[:, LANES:], w["vs"]], axis=0)
    for w in work:
        qp_s = w["rts"] + _dot(w["gb"][LANES:, :LANES], w["w_s"])
        w["qp"] = (qp_s[:CHUNK] + qp_s[CHUNK:]).astype(BF16)
    for w in work:
        y0_s = _dot(w["gb"][LANES:], w["uv"])
        w["y0"] = y0_s[:CHUNK] + y0_s[CHUNK:]
    for w in work:
        w["m"] = _dot(w["bps"], w["w_s"], _TN).astype(BF16)
    for w in work:
        w["n_t"] = _dot(w["uv"], jnp.concatenate([w["bps"], w["kps"]], axis=0), _TN)
    for w in work:
        s = st_ref[w["q"]]
        sb = s.astype(BF16)
        yn_ref[w["rs"], w["ls"]] = _dot(w["qp"], sb, _NT) + w["y0"]
        st_ref[w["q"]] = s * w["p_end"] + _dot(sb, w["m"], _NT) + w["n_t"]

    y = yn_ref[...]
    yc = y - _segsum(y, seg2_ref) * (1.0 / HEAD_DIM)
    var = _segsum(yc * yc, seg2_ref) * (1.0 / HEAD_DIM)
    y = yc * lax.rsqrt(var + GN_EPS) * ln_g + ln_b + bonus
    y_ref[0] = (gate_ref[0] * y).astype(BF16)


def _rwkv(pbs, gate, v_first, prm, vres):
    bsz, t, _ = pbs.shape
    nch = RWKV_CHUNKS
    tb = nch * CHUNK
    has_vres = vres is not None

    tt = jnp.arange(CHUNK)
    tri = (tt[:, None] >= tt[None, :]).astype(BF16)
    tri3 = jnp.concatenate([tri, tri, tri], axis=1)
    idx = jnp.arange(2 * LANES)
    part, blk, pos = idx // LANES, (idx % LANES) // CHUNK, idx % CHUNK
    allowed = (blk[:, None] == blk[None, :]) & (
        (pos[None, :] < pos[:, None]) | ((part[:, None] == 1) & (pos[None, :] == pos[:, None])))
    gmask = allowed.astype(F32)
    eye = jnp.eye(LANES, dtype=F32)

    v0_row = vres[0] if has_vres else jnp.zeros((B_WIDTH,), F32)
    vecs = jnp.stack([prm["w_decay0"], prm["a0"], prm["k_k"], prm["k_a"], prm["r_k"].reshape(-1),
                      prm["lnx_g"], prm["lnx_b"], v0_row])
    consts = [vecs, prm["w_decay2"].astype(BF16), prm["a2"].astype(BF16), _pair_seg2(), tri3, gmask, eye]
    args = [pbs, gate]
    specs = [_row(tb, B_SHIFT_COLS), _row(tb, B_WIDTH)]
    out_specs = [_row(tb, B_WIDTH)]
    out_shape = [jax.ShapeDtypeStruct((bsz, t, B_WIDTH), BF16)]
    if has_vres:
        extra = [vres[1].astype(BF16), vres[2].astype(BF16)]
        args += [v_first] + extra
        specs += [_row(tb, B_WIDTH)] + [_full(a) for a in extra]
    else:
        out_specs.append(_row(tb, B_WIDTH))
        out_shape.append(jax.ShapeDtypeStruct((bsz, t, B_WIDTH), F32))
    args += consts
    specs += [_full(a) for a in consts]
    return pl.pallas_call(
        functools.partial(_rwkv_kernel, has_vres, nch),
        grid=(bsz, t // tb),
        in_specs=specs,
        out_specs=out_specs,
        out_shape=out_shape,
        scratch_shapes=[pltpu.VMEM((tb, B_WIDTH), F32),
                        pltpu.VMEM((PAIRS, LANES, LANES), F32)],
        name="rwkv7_vres" if has_vres else "rwkv7",
        compiler_params=_GRID_PARAMS,
    )(*args)


def _attn_out_kernel(alpha, q_ref, kvp_ref, kvc_ref, gate_ref, e_ref, ya_ref, yb_ref, x_ref, g1_ref,
                     wout_ref, lng_ref, lnb_ref, o_ref, kv_ref, bias_ref, yc_ref):
    tq = ATTN_TQ

    @pl.when((pl.program_id(0) == 0) & (pl.program_id(1) == 0))
    def _():
        key = lax.broadcasted_iota(jnp.int32, (CHUNK, BAND), 1)
        for h in range(C_HEADS):
            rows = jnp.broadcast_to(e_ref[h:h + 1, :], (CHUNK, BIAS_EXT))
            tile = pltpu.roll(rows, 0, 1, stride=1, stride_axis=0)[:, :BAND]
            hs = slice((h % 2) * CHUNK, (h % 2 + 1) * CHUNK)
            bias_ref[0, h // 2, hs, :] = tile
            for c in range(LEFT_CHUNKS):
                bias_ref[1 + c, h // 2, hs, :] = jnp.where(key >= ATTN_LEFT - c * CHUNK, tile, -jnp.inf)

    kv_ref[0:ATTN_LEFT] = kvp_ref[0]
    kv_ref[ATTN_LEFT:ATTN_LEFT + tq] = kvc_ref[0]
    first_block = pl.program_id(1) == 0
    first_head = lax.broadcasted_iota(jnp.int32, (CHUNK, LANES), 1) < HEAD_DIM
    scale = HEAD_DIM ** -0.5

    def scores(c):
        rs = slice(c * CHUNK, (c + 1) * CHUNK)
        band = slice(c * CHUNK, c * CHUNK + BAND)
        table = jnp.where(first_block, 1 + c, 0) if c < LEFT_CHUNKS else 0
        out = []
        for pr in range(PAIRS):
            ls = slice(pr * LANES, (pr + 1) * LANES)
            qs = _stack_heads(q_ref[0, rs, ls] * scale, first_head)
            out.append(_dot(qs, kv_ref[band, ls], _NT) + bias_ref[table, pr])
        return out

    def finish(c, ss):
        rs = slice(c * CHUNK, (c + 1) * CHUNK)
        band = slice(c * CHUNK, c * CHUNK + BAND)
        for pr, s in enumerate(ss):
            ls = slice(pr * LANES, (pr + 1) * LANES)
            vb = kv_ref[band, C_WIDTH + pr * LANES:C_WIDTH + (pr + 1) * LANES]
            e = jnp.exp(s - jnp.max(s, axis=-1, keepdims=True))
            o = _dot(e.astype(BF16), vb) / jnp.sum(e, axis=-1, keepdims=True)
            y = jnp.where(first_head, o[:CHUNK], o[CHUNK:])
            yc_ref[rs, ls] = (gate_ref[0, rs, ls] * y).astype(BF16)

    pending = scores(0)
    for c in range(1, tq // CHUNK):
        nxt = scores(c)
        finish(c - 1, pending)
        pending = nxt
    finish(tq // CHUNK - 1, pending)

    cat = jnp.concatenate([ya_ref[0], yb_ref[0], yc_ref[...]], axis=1)
    z = alpha * x_ref[0] + g1_ref[0] * _dot(cat, wout_ref[...])
    zc = z - jnp.mean(z, axis=-1, keepdims=True)
    var = jnp.mean(zc * zc, axis=-1, keepdims=True)
    o_ref[0] = zc * lax.rsqrt(var + LN_EPS) * lng_ref[...] + lnb_ref[...]


def _attention_output(alpha, q, kv, gate, rel_table, ya, yb, x, gate1p, prm):
    bsz, t, _ = q.shape
    tq = ATTN_TQ
    pad = LEFT_CHUNKS * CHUNK
    m = np.arange(BIAS_EXT)
    dist = np.where(m < BAND, pad - m, pad + BIAS_EXT - m)
    e = rel_table[:, np.clip(dist, -REL_CLIP, REL_CLIP) + REL_CLIP].astype(F32)
    consts = [prm["w_out"].astype(BF16), _vec(prm["ln_g"]), _vec(prm["ln_b"])]
    return pl.pallas_call(
        functools.partial(_attn_out_kernel, alpha),
        grid=(bsz, t // tq),
        in_specs=[
            _row(tq, C_WIDTH),
            pl.BlockSpec((1, ATTN_LEFT, 2 * C_WIDTH),
                         lambda b, i: (b, jnp.maximum(i * (tq // ATTN_LEFT) - 1, 0), 0)),
            _row(tq, 2 * C_WIDTH),
            _row(tq, C_WIDTH),
            _full(e),
            _row(tq, A_WIDTH),
            _row(tq, B_WIDTH),
            _row(tq, D_MODEL),
            pl.BlockSpec((1, 1, D_MODEL), lambda b, i: (b, 0, 0)),
        ] + [_full(a) for a in consts],
        out_specs=_row(tq, D_MODEL),
        out_shape=jax.ShapeDtypeStruct((bsz, t, D_MODEL), F32),
        scratch_shapes=[pltpu.VMEM((ATTN_LEFT + tq, 2 * C_WIDTH), BF16),
                        pltpu.VMEM((1 + LEFT_CHUNKS, PAIRS, 2 * CHUNK, BAND), F32),
                        pltpu.VMEM((tq, C_WIDTH), BF16)],
        name="band_attn_out",
        compiler_params=_GRID_PARAMS,
    )(q, kv, kv, gate, e, ya, yb, x, gate1p, *consts)


def kernel(x, c, w_ada, b_ada, w_in, sgu_ln_g, sgu_ln_b, w_spatial, b_spatial, mu_shift, w_decay0, w_decay2, a0, a2, k_k, k_a, r_k, lnx_g, lnx_b, v0, v1, v2, rel_bias, w_out, ln_g, ln_b):
    depth = w_ada.shape[0]
    alpha = (2 * depth) ** 0.25
    bsz = x.shape[0]
    stacked = dict(sgu_ln_g=sgu_ln_g, sgu_ln_b=sgu_ln_b, w_spatial=w_spatial, b_spatial=b_spatial,
                   mu_shift=mu_shift, w_decay0=w_decay0, w_decay2=w_decay2, a0=a0, a2=a2, k_k=k_k, k_a=k_a,
                   r_k=r_k, lnx_g=lnx_g, lnx_b=lnx_b, w_out=w_out, ln_g=ln_g, ln_b=ln_b)
    v_first = None
    for i in range(depth):
        prm = {name: val[i] for name, val in stacked.items()}
        mod = _modulation(c, w_ada[i], b_ada[i]).reshape(bsz, 3, 1, D_MODEL)
        shift, scale1p, gate1p = mod[:, 0], 1.0 + mod[:, 1], 1.0 + mod[:, 2]
        ya, pbs, gate_b, gate_c, q, kv = _projection(x, scale1p, shift, w_in[i], prm)
        if i == 0:
            yb, v_first = _rwkv(pbs, gate_b, None, prm, None)
        else:
            (yb,) = _rwkv(pbs, gate_b, v_first, prm, (v0[i - 1], v1[i - 1], v2[i - 1]))
        x = _attention_output(alpha, q, kv, gate_c, rel_bias[i], ya, yb, x, gate1p, prm)
    return x
```
